```python
import math
import jax, jax.numpy as jnp
from jax import lax
import numpy as np

D_MODEL = 1024
BATCH = 4
SEQ = 8192
DEPTH = 4

CTX_LEN = 256
GRID_W = 64
HEAD_DIM = 64
D_CONV = 256
CONV_W = 3
H_NA = 6
WIN_R_MAX = 8
WIN_C = 16
H_Q = 6
H_KV = 2
WINDOW = 128
BLK_Q = 128
ROPE_BASE = 10000.0
D_NA = H_NA * HEAD_DIM
D_SW = H_Q * HEAD_DIM
D_KV = H_KV * HEAD_DIM
D_MIX = D_CONV + D_NA + D_SW
N_BRANCH = 3
IN_COLS = 3 * D_CONV + 3 * D_NA + D_SW + 2 * D_KV + N_BRANCH * D_MODEL
N_GROUPS = 4
EXP_PER_GROUP = 8
N_EXPERTS = N_GROUPS * EXP_PER_GROUP
TOP_K = 2
D_EXPERT = 256
MOE_BLK = 256
EPS = 1e-6
NEG_INF = -1e30

kernel_name = "hybrid_conv_natten_swa_hmoe_dit"


def rmsnorm(x, g):
    xf = x.astype(jnp.float32)
    y = xf * lax.rsqrt(jnp.mean(xf * xf, axis=-1, keepdims=True) + EPS)
    return (y * g.astype(jnp.float32)).astype(x.dtype)


def modln(x, g, shift, scale):
    return rmsnorm(x, g) * (1 + scale) + shift


def heads(t, n):
    return t.reshape(t.shape[:-1] + (n, HEAD_DIM))


def split_in(u):
    sizes = (D_CONV, D_CONV, D_CONV, D_NA, D_NA, D_NA, D_SW, D_KV, D_KV, N_BRANCH * D_MODEL)
    idx, acc = [], 0
    for s in sizes[:-1]:
        acc += s
        idx.append(acc)
    return jnp.split(u, idx, axis=-1)


def axial_rope(x, row, col):
    half = HEAD_DIM // 2
    quarter = half // 2
    inv = ROPE_BASE ** (-jnp.arange(quarter, dtype=jnp.float32) / quarter)

    def rot(xa, pos):
        ang = pos[:, None] * inv[None, :]
        cos = jnp.cos(ang)[:, None, :]
        sin = jnp.sin(ang)[:, None, :]
        x1, x2 = jnp.split(xa, 2, axis=-1)
        return jnp.concatenate([x1 * cos - x2 * sin, x1 * sin + x2 * cos], axis=-1)

    xf = x.astype(jnp.float32)
    out = jnp.concatenate([rot(xf[..., :half], row), rot(xf[..., half:], col)], axis=-1)
    return out.astype(x.dtype)


def short_conv(h, b_gate, c_gate, w):
    z = c_gate * h
    pad = CONV_W // 2
    n = z.shape[1]
    zp = jnp.pad(z, ((0, 0), (pad, pad), (0, 0)))
    y = zp[:, 0:n] * w[0]
    for j in range(1, CONV_W):
        y = y + zp[:, j:j + n] * w[j]
    return b_gate * y


def ctx_attn(q, k, v, sink):
    bsz, n, hq, dh = q.shape
    hkv = k.shape[2]
    g = hq // hkv
    qg = q.reshape(bsz, n, hkv, g, dh)
    s = jnp.einsum('bqkgd,bjkd->bkgqj', qg, k).astype(jnp.float32) * (dh ** -0.5)
    if sink is not None:
        sink_col = jnp.broadcast_to(sink.astype(jnp.float32).reshape(1, hkv, g, 1, 1), (bsz, hkv, g, n, 1))
        s = jnp.concatenate([s, sink_col], axis=-1)
    p = jax.nn.softmax(s, axis=-1)[..., :n].astype(v.dtype)
    o = jnp.einsum('bkgqj,bjkd->bqkgd', p, v)
    return o.reshape(bsz, n, hq * dh)


def neighbourhood_attn(q, k, v, kc, vc, rpb):
    bsz, n, h, dh = q.shape
    rows = n // GRID_W
    win_r = min(WIN_R_MAX, rows)
    nw = win_r * WIN_C
    scale = dh ** -0.5
    kg = k.reshape(bsz, rows, GRID_W, h, dh)
    vg = v.reshape(bsz, rows, GRID_W, h, dh)
    qg = q.reshape(bsz, rows, GRID_W, h, dh).transpose(1, 0, 2, 3, 4)
    cols = jnp.arange(GRID_W)
    cstart = jnp.clip(cols - WIN_C // 2, 0, GRID_W - WIN_C)
    col_idx = cstart[:, None] + jnp.arange(WIN_C)[None, :]
    rel_c = col_idx - cols[:, None] + (WIN_C - 1)

    def one_row(args):
        r, q_row = args
        rs = jnp.clip(r - win_r // 2, 0, rows - win_r)
        kb = lax.dynamic_slice_in_dim(kg, rs, win_r, axis=1)
        vb = lax.dynamic_slice_in_dim(vg, rs, win_r, axis=1)
        kw = kb[:, :, col_idx]
        vw = vb[:, :, col_idx]
        rel_r = rs + jnp.arange(win_r) - r + (WIN_R_MAX - 1)
        bias = rpb[:, rel_r[None, :, None], rel_c[:, None, :]]
        s_win = jnp.einsum('bqhd,baqchd->bhqac', q_row, kw).astype(jnp.float32) * scale
        s_win = (s_win + bias.astype(jnp.float32)[None]).reshape(bsz, h, GRID_W, nw)
        s_ctx = jnp.einsum('bqhd,bnhd->bhqn', q_row, kc).astype(jnp.float32) * scale
        p = jax.nn.softmax(jnp.concatenate([s_win, s_ctx], axis=-1), axis=-1).astype(q.dtype)
        p_win = p[..., :nw].reshape(bsz, h, GRID_W, win_r, WIN_C)
        o = jnp.einsum('bhqac,baqchd->bqhd', p_win, vw) + jnp.einsum('bhqn,bnhd->bqhd', p[..., nw:], vc)
        return o

    out = lax.map(one_row, (jnp.arange(rows), qg))
    return out.transpose(1, 0, 2, 3, 4).reshape(bsz, n, h * dh)


def window_attn(q, k, v, kc, vc, sink):
    bsz, n, hq, dh = q.shape
    hkv = k.shape[2]
    g = hq // hkv
    nctx = kc.shape[1]
    nblk = n // BLK_Q
    span = BLK_Q + 2 * WINDOW
    scale = dh ** -0.5
    kp = jnp.pad(k, ((0, 0), (WINDOW, WINDOW), (0, 0), (0, 0)))
    vp = jnp.pad(v, ((0, 0), (WINDOW, WINDOW), (0, 0), (0, 0)))
    qb = q.reshape(bsz, nblk, BLK_Q, hkv, g, dh).transpose(1, 0, 2, 3, 4, 5)
    sink_col = jnp.broadcast_to(sink.astype(jnp.float32).reshape(1, hkv, g, 1, 1), (bsz, hkv, g, BLK_Q, 1))

    def one_block(args):
        b, q_blk = args
        start = b * BLK_Q
        kw = lax.dynamic_slice_in_dim(kp, start, span, axis=1)
        vw = lax.dynamic_slice_in_dim(vp, start, span, axis=1)
        qpos = start + jnp.arange(BLK_Q)
        kpos = start - WINDOW + jnp.arange(span)
        valid = (jnp.abs(qpos[:, None] - kpos[None, :]) <= WINDOW) & (kpos >= 0)[None, :] & (kpos < n)[None, :]
        s_win = jnp.einsum('bqkgd,bjkd->bkgqj', q_blk, kw).astype(jnp.float32) * scale
        s_win = jnp.where(valid, s_win, NEG_INF)
        s_ctx = jnp.einsum('bqkgd,bckd->bkgqc', q_blk, kc).astype(jnp.float32) * scale
        p = jax.nn.softmax(jnp.concatenate([s_win, s_ctx, sink_col], axis=-1), axis=-1).astype(q.dtype)
        o = jnp.einsum('bkgqj,bjkd->bqkgd', p[..., :span], vw) + jnp.einsum('bkgqc,bckd->bqkgd', p[..., span:span + nctx], vc)
        return o.reshape(bsz, BLK_Q, hq * dh)

    out = lax.map(one_block, (jnp.arange(nblk), qb))
    return out.transpose(1, 0, 2, 3).reshape(bsz, n, hq * dh)


def merge_branches(y_a, y_b, y_c, gate_logits, w_branch, w_o):
    g_a, g_b, g_c = jnp.split(jax.nn.sigmoid(gate_logits), N_BRANCH, axis=-1)
    o1 = D_CONV
    o2 = D_CONV + D_NA
    m = g_a * (y_a @ w_branch[:o1]) + g_b * (y_b @ w_branch[o1:o2]) + g_c * (y_c @ w_branch[o2:])
    return m @ w_o


def token_mixers(h, hc, w_in, conv_w, rpb, sink, w_branch, w_o, row, col, need_ctx):
    a_h, a_b, a_c, nq, nk, nv, wq, wk, wv, gates = split_in(h @ w_in)
    ca_h, ca_b, ca_c, cnq, cnk, cnv, cwq, cwk, cwv, cgates = split_in(hc @ w_in)
    cnk, cnv = heads(cnk, H_NA), heads(cnv, H_NA)
    cwk, cwv = heads(cwk, H_KV), heads(cwv, H_KV)
    y_a = short_conv(a_h, a_b, a_c, conv_w)
    y_b = neighbourhood_attn(heads(nq, H_NA), heads(nk, H_NA), heads(nv, H_NA), cnk, cnv, rpb)
    y_c = window_attn(axial_rope(heads(wq, H_Q), row, col), axial_rope(heads(wk, H_KV), row, col),
                      heads(wv, H_KV), cwk, cwv, sink)
    y = merge_branches(y_a, y_b, y_c, gates, w_branch, w_o)
    if not need_ctx:
        return y, None
    yc_a = short_conv(ca_h, ca_b, ca_c, conv_w)
    yc_b = ctx_attn(heads(cnq, H_NA), cnk, cnv, None)
    yc_c = ctx_attn(heads(cwq, H_Q), cwk, cwv, sink)
    yc = merge_branches(yc_a, yc_b, yc_c, cgates, w_branch, w_o)
    return y, yc


def hier_moe(xt, w_rg, b_rg, w_re, b_re, w_g, w_u, w_d):
    n, d = xt.shape
    lg = (xt @ w_rg + b_rg).astype(jnp.float32)
    pg = jax.nn.softmax(lg, axis=-1)
    gi = jnp.argmax(lg, axis=-1)
    pg_top = jnp.take_along_axis(pg, gi[:, None], axis=-1)[:, 0]
    le = (xt @ w_re + b_re).astype(jnp.float32).reshape(n, N_GROUPS, EXP_PER_GROUP)
    le_g = jnp.take_along_axis(le, gi[:, None, None], axis=1)[:, 0]
    top_v, top_i = lax.top_k(le_g, TOP_K)
    wts = (pg_top[:, None] * jax.nn.softmax(top_v, axis=-1)).reshape(-1)
    eid = (gi[:, None] * EXP_PER_GROUP + top_i).reshape(-1).astype(jnp.int32)
    tok = jnp.repeat(jnp.arange(n), TOP_K)
    a = n * TOP_K
    order = jnp.argsort(eid)
    e_s, t_s, w_s = eid[order], tok[order], wts[order]
    counts = jnp.bincount(eid, length=N_EXPERTS)
    padded = ((counts + MOE_BLK - 1) // MOE_BLK) * MOE_BLK
    seg_start = jnp.cumsum(counts) - counts
    pad_end = jnp.cumsum(padded)
    pad_start = pad_end - padded
    dest = pad_start[e_s] + jnp.arange(a) - seg_start[e_s]
    nb = (a + N_EXPERTS * (MOE_BLK - 1) + MOE_BLK - 1) // MOE_BLK
    xd = jnp.zeros((nb * MOE_BLK, d), xt.dtype).at[dest].set(xt[t_s])
    blk_start = jnp.arange(nb) * MOE_BLK
    blk_e = jnp.clip(jnp.sum(pad_end[None, :] <= blk_start[:, None], axis=1), 0, N_EXPERTS - 1)

    def expert_block(args):
        xb, e = args
        hid = jax.nn.silu(xb @ w_g[e]) * (xb @ w_u[e])
        return hid @ w_d[e]

    yd = lax.map(expert_block, (xd.reshape(nb, MOE_BLK, d), blk_e)).reshape(nb * MOE_BLK, d)
    return jnp.zeros((n, d), xt.dtype).at[t_s].add(yd[dest] * w_s[:, None].astype(xt.dtype))


def setup_inputs(seed: int = 0) -> dict:
    key = jax.random.key(seed)
    ks = jax.random.split(key, 26)
    f32 = jnp.float32

    def nrm(k, shape, scale):
        return jax.random.normal(k, shape, f32) * scale

    w_branch = jnp.concatenate([
        nrm(ks[12], (DEPTH, D_CONV, D_MODEL), D_CONV ** -0.5),
        nrm(ks[13], (DEPTH, D_NA, D_MODEL), D_NA ** -0.5),
        nrm(ks[14], (DEPTH, D_SW, D_MODEL), D_SW ** -0.5)], axis=1)
    return {
        "x": nrm(ks[0], (BATCH, SEQ, D_MODEL), 1.0),
        "c": nrm(ks[1], (BATCH, D_MODEL), 1.0),
        "ctx": nrm(ks[2], (BATCH, CTX_LEN, D_MODEL), 1.0),
        "c_ctx": nrm(ks[3], (D_MODEL,), 1.0),
        "w_mod": nrm(ks[4], (DEPTH, D_MODEL, 6 * D_MODEL), 0.5 * D_MODEL ** -0.5),
        "b_mod": nrm(ks[5], (DEPTH, 6 * D_MODEL), 0.02),
        "g_attn": 1.0 + nrm(ks[6], (DEPTH, D_MODEL), 0.05),
        "g_ffn": 1.0 + nrm(ks[7], (DEPTH, D_MODEL), 0.05),
        "w_in": nrm(ks[8], (DEPTH, D_MODEL, IN_COLS), D_MODEL ** -0.5),
        "conv_w": nrm(ks[9], (DEPTH, CONV_W, D_CONV), CONV_W ** -0.5),
        "rpb": nrm(ks[10], (DEPTH, H_NA, 2 * WIN_R_MAX - 1, 2 * WIN_C - 1), 0.1),
        "sinks": nrm(ks[11], (DEPTH, H_Q), 0.5),
        "w_branch": w_branch,
        "w_o": nrm(ks[15], (DEPTH, D_MODEL, D_MODEL), D_MODEL ** -0.5),
        "w_rg": nrm(ks[16], (DEPTH, D_MODEL, N_GROUPS), D_MODEL ** -0.5),
        "b_rg": nrm(ks[17], (DEPTH, N_GROUPS), 0.01),
        "w_re": nrm(ks[18], (DEPTH, D_MODEL, N_EXPERTS), D_MODEL ** -0.5),
        "b_re": nrm(ks[19], (DEPTH, N_EXPERTS), 0.01),
        "w_e_gate": nrm(ks[20], (DEPTH, N_EXPERTS, D_MODEL, D_EXPERT), D_MODEL ** -0.5),
        "w_e_up": nrm(ks[21], (DEPTH, N_EXPERTS, D_MODEL, D_EXPERT), D_MODEL ** -0.5),
        "w_e_down": nrm(ks[22], (DEPTH, N_EXPERTS, D_EXPERT, D_MODEL), D_EXPERT ** -0.5),
        "g_final": 1.0 + nrm(ks[23], (D_MODEL,), 0.05),
    }


def reference(x, c, ctx, c_ctx, w_mod, b_mod, g_attn, g_ffn, w_in, conv_w, rpb, sinks,
              w_branch, w_o, w_rg, b_rg, w_re, b_re, w_e_gate, w_e_up, w_e_down, g_final):
    bsz, n, d = x.shape
    t = jnp.arange(n)
    row = (t // GRID_W).astype(jnp.float32)
    col = (t % GRID_W).astype(jnp.float32)
    silu_c = jax.nn.silu(c)
    silu_cc = jax.nn.silu(c_ctx)
    xc = ctx
    for l in range(DEPTH):
        need_ctx = l < DEPTH - 1
        mod = (silu_c @ w_mod[l] + b_mod[l])[:, None, :]
        modc = silu_cc @ w_mod[l] + b_mod[l]
        sh_a, sc_a, gt_a, sh_f, sc_f, gt_f = jnp.split(mod, 6, axis=-1)
        csh_a, csc_a, cgt_a, csh_f, csc_f, cgt_f = jnp.split(modc, 6, axis=-1)
        h = modln(x, g_attn[l], sh_a, sc_a)
        hc = modln(xc, g_attn[l], csh_a, csc_a)
        y, yc = token_mixers(h, hc, w_in[l], conv_w[l], rpb[l], sinks[l], w_branch[l], w_o[l], row, col, need_ctx)
        x = x + gt_a * y
        h2 = modln(x, g_ffn[l], sh_f, sc_f)
        if need_ctx:
            xc = xc + cgt_a * yc
            h2c = modln(xc, g_ffn[l], csh_f, csc_f)
            toks = jnp.concatenate([h2.reshape(-1, d), h2c.reshape(-1, d)], axis=0)
            yt = hier_moe(toks, w_rg[l], b_rg[l], w_re[l], b_re[l], w_e_gate[l], w_e_up[l], w_e_down[l])
            nl = bsz * n
            x = x + gt_f * yt[:nl].reshape(x.shape)
            xc = xc + cgt_f * yt[nl:].reshape(xc.shape)
        else:
            yt = hier_moe(h2.reshape(-1, d), w_rg[l], b_rg[l], w_re[l], b_re[l], w_e_gate[l], w_e_up[l], w_e_down[l])
            x = x + gt_f * yt.reshape(x.shape)
    return rmsnorm(x, g_final)
```

```python
import functools

import numpy as np
import jax
import jax.numpy as jnp
from jax import lax
from jax.experimental import pallas as pl
from jax.experimental.pallas import tpu as pltpu

F32 = jnp.float32
BF16 = jnp.bfloat16

D_MODEL = 1024
GRID_W = 64
HEAD_DIM = 64
D_CONV = 256
CONV_W = 3
H_NA = 6
WIN_R = 8
WIN_C = 16
H_Q = 6
H_KV = 2
WINDOW = 128
ROPE_BASE = 10000.0
D_NA = H_NA * HEAD_DIM
D_SW = H_Q * HEAD_DIM
D_KV = H_KV * HEAD_DIM
N_GROUPS = 4
EXP_PER_GROUP = 8
N_EXPERTS = N_GROUPS * EXP_PER_GROUP
D_EXPERT = 256
MOE_BLK = 256
EPS = 1e-6
NEG_INF = -1e30

LANES = 128
TOK_TILE = 256
NA_ROWS = TOK_TILE // GRID_W
NA_KROWS = NA_ROWS + WIN_R - 1
NA_KEYS = NA_KROWS * GRID_W
SW_KEYS = TOK_TILE + 2 * WINDOW
VMEM_LIMIT = 56 * 1024 * 1024

_C_CONV = 0
_C_NQ = _C_CONV + 3 * D_CONV
_C_NK = _C_NQ + D_NA
_C_NV = _C_NK + D_NA
_C_WQ = _C_NV + D_NA
_C_WQS = _C_WQ + D_SW
_C_WK = _C_WQS + D_SW
_C_WKS = _C_WK + D_KV
_C_WV = _C_WKS + D_KV
_C_GATE = _C_WV + D_KV
_C_END = _C_GATE + 3 * D_MODEL

_SW_HEAD_ORDER = (0, 3, 1, 4, 2, 5)


def _dot(a, b):
    return jnp.dot(a, b, preferred_element_type=F32)


def _dot_nt(a, b):
    return lax.dot_general(a, b, (((1,), (1,)), ((), ())), preferred_element_type=F32)


def _sigmoid(x):
    return 1.0 / (1.0 + jnp.exp(-x))


def _modln(x, g, shift, scale):
    ms = jnp.mean(x * x, axis=-1, keepdims=True)
    return (x * lax.rsqrt(ms + EPS)) * g * (1.0 + scale) + shift


def _mod_kernel(c_ref, w_ref, b_ref, o_ref):
    cv = c_ref[...]
    sc = (cv * _sigmoid(cv)).astype(BF16)
    o_ref[0] = _dot(sc, w_ref[0].astype(BF16)) + b_ref[0]


def _mod_vectors(c8, w_mod, b_mod):
    depth, d, cols = w_mod.shape
    tn = cols // 4
    return pl.pallas_call(
        _mod_kernel,
        grid=(depth, cols // tn),
        in_specs=[
            pl.BlockSpec((8, d), lambda l, j: (0, 0)),
            pl.BlockSpec((1, d, tn), lambda l, j: (l, 0, j)),
            pl.BlockSpec((1, 1, tn), lambda l, j: (l, 0, j)),
        ],
        out_specs=pl.BlockSpec((1, 8, tn), lambda l, j: (l, 0, j)),
        out_shape=jax.ShapeDtypeStruct((depth, 8, cols), F32),
        compiler_params=pltpu.CompilerParams(
            dimension_semantics=("arbitrary", "arbitrary"), vmem_limit_bytes=VMEM_LIMIT),
        name="mod_vectors",
    )(c8, w_mod, b_mod.reshape(depth, 1, cols))


def _inproj_kernel(x_ref, mod_ref, g_ref, w_ref, cos_ref, sin_ref,
                   z_ref, ab_ref, nq_ref, nk_ref, nv_ref, wq_ref, wk_ref, wv_ref, gate_ref):
    x = x_ref[0]
    h = _modln(x, g_ref[...], mod_ref[0, 0:1, :], mod_ref[0, 1:2, :]).astype(BF16)

    def proj(c0, c1):
        return _dot(h, w_ref[:, c0:c1])

    a_h = proj(_C_CONV, _C_CONV + D_CONV)
    a_b = proj(_C_CONV + D_CONV, _C_CONV + 2 * D_CONV)
    a_c = proj(_C_CONV + 2 * D_CONV, _C_NQ)
    z_ref[0] = (a_c * a_h).astype(BF16)
    ab_ref[0] = a_b.astype(BF16)

    qk_scale = HEAD_DIM ** -0.5
    nq_ref[0] = (proj(_C_NQ, _C_NK) * qk_scale).astype(BF16)
    nk_ref[0] = proj(_C_NK, _C_NV).astype(BF16)
    nv_ref[0] = proj(_C_NV, _C_WQ).astype(BF16)

    cos = cos_ref[...]
    sin = sin_ref[...]
    cos3 = jnp.concatenate([cos] * (D_SW // LANES), axis=-1)
    sin3 = jnp.concatenate([sin] * (D_SW // LANES), axis=-1)
    wq = proj(_C_WQ, _C_WQS) * cos3 + proj(_C_WQS, _C_WK) * sin3
    wq_ref[0] = (wq * qk_scale).astype(BF16)
    wk_ref[0] = (proj(_C_WK, _C_WKS) * cos + proj(_C_WKS, _C_WV) * sin).astype(BF16)
    wv_ref[0] = proj(_C_WV, _C_GATE).astype(BF16)

    for j in range(3):
        c0 = _C_GATE + j * D_MODEL
        gate_ref[0, :, j * D_MODEL:(j + 1) * D_MODEL] = _sigmoid(proj(c0, c0 + D_MODEL)).astype(BF16)


def _inproj(x_all, mod8, g, w, cos_t, sin_t, n_lat):
    bsz, ltot, d = x_all.shape
    n_tiles = ltot // TOK_TILE
    tm = TOK_TILE

    def tok_spec(width):
        return pl.BlockSpec((1, tm, width), lambda b, i: (b, i, 0))

    def mod_map(b, i):
        return (jnp.where(i < n_lat, b, bsz), 0, 0)

    widths = (D_CONV, D_CONV, D_NA, D_NA, D_NA, D_SW, D_KV, D_KV, 3 * D_MODEL)
    return pl.pallas_call(
        _inproj_kernel,
        grid=(bsz, n_tiles),
        in_specs=[
            tok_spec(d),
            pl.BlockSpec((1, 8, d), mod_map),
            pl.BlockSpec((1, d), lambda b, i: (0, 0)),
            pl.BlockSpec((d, _C_END), lambda b, i: (0, 0), pipeline_mode=pl.Buffered(1)),
            pl.BlockSpec((tm, LANES), lambda b, i: (i, 0)),
            pl.BlockSpec((tm, LANES), lambda b, i: (i, 0)),
        ],
        out_specs=[tok_spec(wd) for wd in widths],
        out_shape=[jax.ShapeDtypeStruct((bsz, ltot, wd), BF16) for wd in widths],
        compiler_params=pltpu.CompilerParams(
            dimension_semantics=("arbitrary", "arbitrary"), vmem_limit_bytes=VMEM_LIMIT),
        name="inproj",
    )(x_all, mod8, g, w, cos_t, sin_t)


def _half_mask(half):
    lane = lax.broadcasted_iota(jnp.int32, (1, LANES), 1)
    return (lane < HEAD_DIM) if half == 0 else (lane >= HEAD_DIM)


def _softmax_pv(parts, sink=None):
    m = parts[0][0].max(axis=-1, keepdims=True)
    for s, _ in parts[1:]:
        m = jnp.maximum(m, s.max(axis=-1, keepdims=True))
    if sink is not None:
        m = jnp.maximum(m, sink)
    l = None
    o = None
    for s, v in parts:
        e = jnp.exp(s - m)
        ls = e.sum(axis=-1, keepdims=True)
        os_ = _dot(e.astype(BF16), v)
        l = ls if l is None else l + ls
        o = os_ if o is None else o + os_
    if sink is not None:
        l = l + jnp.exp(sink - m)
    return o * (1.0 / l)


def _na_kernel(q_ref, k_ref, v_ref, bias_ref, o_ref, *, n_blk, rows, seq):
    i = pl.program_id(1)
    ctx_len = k_ref.shape[1] - seq

    def pair_slices(p):
        return slice(p * LANES, (p + 1) * LANES)

    @pl.when(i < n_blk)
    def _latent():
        r0 = i * NA_ROWS
        ks = jnp.clip(r0 - WIN_R // 2, 0, rows - NA_KROWS)
        kstart = pl.multiple_of(ks * GRID_W, GRID_W)
        for p in range(H_NA // 2):
            ps = pair_slices(p)
            q2 = q_ref[0, :, ps]
            k2 = k_ref[0, pl.ds(kstart, NA_KEYS), ps]
            v2 = v_ref[0, pl.ds(kstart, NA_KEYS), ps]
            kc2 = k_ref[0, seq:seq + ctx_len, ps]
            vc2 = v_ref[0, seq:seq + ctx_len, ps]
            outs = []
            for half in range(2):
                qm = jnp.where(_half_mask(half), q2, jnp.zeros_like(q2))
                s = _dot_nt(qm, k2) + bias_ref[0, 2 * p + half]
                sc = _dot_nt(qm, kc2)
                outs.append(_softmax_pv([(s, v2), (sc, vc2)]))
            o_ref[0, :, ps] = jnp.where(_half_mask(0), outs[0], outs[1]).astype(BF16)

    @pl.when(i == n_blk)
    def _context():
        for p in range(H_NA // 2):
            ps = pair_slices(p)
            q2 = q_ref[0, :, ps]
            kc2 = k_ref[0, seq:seq + ctx_len, ps]
            vc2 = v_ref[0, seq:seq + ctx_len, ps]
            outs = []
            for half in range(2):
                qm = jnp.where(_half_mask(half), q2, jnp.zeros_like(q2))
                outs.append(_softmax_pv([(_dot_nt(qm, kc2), vc2)]))
            o_ref[0, :, ps] = jnp.where(_half_mask(0), outs[0], outs[1]).astype(BF16)


def _na_attn(nq, nk, nv, bias, seq):
    bsz, ltot, _ = nq.shape
    rows = seq // GRID_W
    n_blk = seq // TOK_TILE
    n_steps = ltot // TOK_TILE

    def bias_map(b, i):
        return (jnp.where(i == 0, 0, jnp.where(i >= n_blk - 1, 2, 1)), 0, 0, 0)

    return pl.pallas_call(
        functools.partial(_na_kernel, n_blk=n_blk, rows=rows, seq=seq),
        grid=(bsz, n_steps),
        in_specs=[
            pl.BlockSpec((1, TOK_TILE, D_NA), lambda b, i: (b, i, 0)),
            pl.BlockSpec((1, ltot, D_NA), lambda b, i: (b, 0, 0)),
            pl.BlockSpec((1, ltot, D_NA), lambda b, i: (b, 0, 0)),
            pl.BlockSpec((1, H_NA, TOK_TILE, NA_KEYS), bias_map),
        ],
        out_specs=pl.BlockSpec((1, TOK_TILE, D_NA), lambda b, i: (b, i, 0)),
        out_shape=jax.ShapeDtypeStruct((bsz, ltot, D_NA), BF16),
        compiler_params=pltpu.CompilerParams(
            dimension_semantics=("arbitrary", "arbitrary"), vmem_limit_bytes=VMEM_LIMIT),
        name="na_attn",
    )(nq, nk, nv, bias)


def _sw_kernel(sink_ref, q_ref, k_ref, v_ref, o_ref, *, n_blk, seq):
    i = pl.program_id(1)
    ctx_len = k_ref.shape[1] - seq
    kc2 = k_ref[0, seq:seq + ctx_len, :]
    vc2 = v_ref[0, seq:seq + ctx_len, :]

    def heads_of_pair(p, window_parts):
        q2 = q_ref[0, :, p * LANES:(p + 1) * LANES]
        outs = []
        for half in range(2):
            sink = sink_ref[half * (H_Q // H_KV) + p]
            qm = jnp.where(_half_mask(half), q2, jnp.zeros_like(q2))
            parts = [(fn(qm), v) for fn, v in window_parts] + [(_dot_nt(qm, kc2), vc2)]
            outs.append(_softmax_pv(parts, sink=sink))
        o_ref[0, :, p * LANES:(p + 1) * LANES] = jnp.where(_half_mask(0), outs[0], outs[1]).astype(BF16)

    @pl.when(i < n_blk)
    def _latent():
        t0 = i * TOK_TILE
        kst = pl.multiple_of(jnp.clip(t0 - WINDOW, 0, seq - SW_KEYS), WINDOW)
        qpos = t0 + lax.broadcasted_iota(jnp.int32, (TOK_TILE, 1), 0)
        kpos = kst + lax.broadcasted_iota(jnp.int32, (1, SW_KEYS), 1)
        valid = jnp.abs(qpos - kpos) <= WINDOW
        k2 = k_ref[0, pl.ds(kst, SW_KEYS), :]
        v2 = v_ref[0, pl.ds(kst, SW_KEYS), :]
        for p in range(H_Q // 2):
            heads_of_pair(p, [(lambda qm: jnp.where(valid, _dot_nt(qm, k2), NEG_INF), v2)])

    @pl.when(i == n_blk)
    def _context():
        for p in range(H_Q // 2):
            heads_of_pair(p, [])


def _sw_attn(sinks, wq, wk, wv, seq):
    bsz, ltot, _ = wq.shape
    n_blk = seq // TOK_TILE
    n_steps = ltot // TOK_TILE
    return pl.pallas_call(
        functools.partial(_sw_kernel, n_blk=n_blk, seq=seq),
        grid_spec=pltpu.PrefetchScalarGridSpec(
            num_scalar_prefetch=1,
            grid=(bsz, n_steps),
            in_specs=[
                pl.BlockSpec((1, TOK_TILE, D_SW), lambda b, i, s: (b, i, 0)),
                pl.BlockSpec((1, ltot, D_KV), lambda b, i, s: (b, 0, 0)),
                pl.BlockSpec((1, ltot, D_KV), lambda b, i, s: (b, 0, 0)),
            ],
            out_specs=pl.BlockSpec((1, TOK_TILE, D_SW), lambda b, i, s: (b, i, 0)),
        ),
        out_shape=jax.ShapeDtypeStruct((bsz, ltot, D_SW), BF16),
        compiler_params=pltpu.CompilerParams(
            dimension_semantics=("arbitrary", "arbitrary"), vmem_limit_bytes=VMEM_LIMIT),
        name="sw_attn",
    )(sinks, wq, wk, wv)


def _split_bf16(x):
    hi = x.astype(BF16)
    lo = (x - hi.astype(F32)).astype(BF16)
    return hi, lo


def _merge_kernel(x_ref, mod_ref, g_ref, z_ref, zp_ref, zn_ref, ab_ref, yb_ref, yc_ref, gate_ref,
                  convw_ref, wb_ref, wo_ref, wr_hi_ref, wr_lo_ref, br_ref,
                  xo_ref, h2_ref, rt_ref, *, n_lat):
    i = pl.program_id(1)
    n_tiles = pl.num_programs(1)
    tm = x_ref.shape[1]

    z = z_ref[0].astype(F32)
    halo = zp_ref.shape[1]
    first = (i == 0) | (i == n_lat)
    last = (i == n_lat - 1) | (i == n_tiles - 1)
    prev_row = jnp.where(first, 0.0, zp_ref[0, halo - 1:halo, :].astype(F32))
    next_row = jnp.where(last, 0.0, zn_ref[0, 0:1, :].astype(F32))
    row = lax.broadcasted_iota(jnp.int32, (tm, 1), 0)
    z_prev = jnp.where(row == 0, prev_row, pltpu.roll(z, 1, axis=0))
    z_next = jnp.where(row == tm - 1, next_row, pltpu.roll(z, tm - 1, axis=0))
    cw = convw_ref[...]
    y_a = ab_ref[0].astype(F32) * (z_prev * cw[0:1] + z * cw[1:2] + z_next * cw[2:3])

    o1 = D_CONV
    o2 = D_CONV + D_NA
    g_a = gate_ref[0, :, 0:D_MODEL].astype(F32)
    g_b = gate_ref[0, :, D_MODEL:2 * D_MODEL].astype(F32)
    g_c = gate_ref[0, :, 2 * D_MODEL:3 * D_MODEL].astype(F32)
    m = (g_a * _dot(y_a.astype(BF16), wb_ref[0:o1, :])
         + g_b * _dot(yb_ref[0], wb_ref[o1:o2, :])
         + g_c * _dot(yc_ref[0], wb_ref[o2:, :]))
    y = _dot(m.astype(BF16), wo_ref[...])
    x_new = x_ref[0] + mod_ref[0, 2:3, :] * y
    xo_ref[0] = x_new

    h2 = _modln(x_new, g_ref[...], mod_ref[0, 3:4, :], mod_ref[0, 4:5, :])
    h2_ref[0] = h2

    h_hi, h_lo = _split_bf16(h2)
    lg = (_dot(h_hi, wr_hi_ref[...]) + _dot(h_hi, wr_lo_ref[...]) + _dot(h_lo, wr_hi_ref[...])
          + br_ref[...])
    lane = lax.broadcasted_iota(jnp.int32, lg.shape, 1)
    big = jnp.int32(LANES)

    def masked_argmax(vals, mask):
        v = jnp.where(mask, vals, -jnp.inf)
        vmax = v.max(axis=-1, keepdims=True)
        idx = jnp.where(mask & (v == vmax), lane, big).min(axis=-1, keepdims=True)
        return vmax, idx

    is_grp = lane < N_GROUPS
    g_max, g_idx = masked_argmax(lg, is_grp)
    pg_top = 1.0 / jnp.where(is_grp, jnp.exp(lg - g_max), 0.0).sum(axis=-1, keepdims=True)
    e_lo = 8 + g_idx * EXP_PER_GROUP
    in_grp = (lane >= e_lo) & (lane < e_lo + EXP_PER_GROUP)
    v1, i1 = masked_argmax(lg, in_grp)
    v2, i2 = masked_argmax(lg, in_grp & (lane != i1))
    e21 = jnp.exp(v2 - v1)
    w1 = pg_top / (1.0 + e21)
    w2 = pg_top * e21 / (1.0 + e21)
    rt = jnp.where(lane == 0, (i1 - 8).astype(F32),
                   jnp.where(lane == 1, (i2 - 8).astype(F32),
                             jnp.where(lane == 2, w1, jnp.where(lane == 3, w2, 0.0))))
    rt_ref[0] = rt


def _merge(x_all, mod8, g, z, ab, y_b, y_c, gates, conv_w, wb, wo, wr_hi, wr_lo, br, n_lat):
    bsz, ltot, d = x_all.shape
    n_tiles = ltot // TOK_TILE
    tm = TOK_TILE
    halo = 16
    hb = tm // halo
    n_halo = ltot // halo

    def tok_spec(width):
        return pl.BlockSpec((1, tm, width), lambda b, i: (b, i, 0))

    def mod_map(b, i):
        return (jnp.where(i < n_lat, b, bsz), 0, 0)

    def const2(shape):
        return pl.BlockSpec(shape, lambda b, i: (0, 0))

    return pl.pallas_call(
        functools.partial(_merge_kernel, n_lat=n_lat),
        grid=(bsz, n_tiles),
        in_specs=[
            tok_spec(d),
            pl.BlockSpec((1, 8, d), mod_map),
            const2((1, d)),
            tok_spec(D_CONV),
            pl.BlockSpec((1, halo, D_CONV), lambda b, i: (b, jnp.maximum(i * hb - 1, 0), 0)),
            pl.BlockSpec((1, halo, D_CONV), lambda b, i: (b, jnp.minimum((i + 1) * hb, n_halo - 1), 0)),
            tok_spec(D_CONV),
            tok_spec(D_NA),
            tok_spec(D_SW),
            tok_spec(3 * d),
            const2((8, D_CONV)),
            const2((d, d)),
            const2((d, d)),
            const2((d, LANES)),
            const2((d, LANES)),
            const2((1, LANES)),
        ],
        out_specs=[tok_spec(d), tok_spec(d), tok_spec(LANES)],
        out_shape=[jax.ShapeDtypeStruct((bsz, ltot, d), F32),
                   jax.ShapeDtypeStruct((bsz, ltot, d), F32),
                   jax.ShapeDtypeStruct((bsz, ltot, LANES), F32)],
        compiler_params=pltpu.CompilerParams(
            dimension_semantics=("arbitrary", "arbitrary"), vmem_limit_bytes=VMEM_LIMIT),
        name="merge",
    )(x_all, mod8, g, z, z, z, ab, y_b, y_c, gates, conv_w, wb, wo, wr_hi, wr_lo, br)


def _moe_kernel(blk_e_ref, nused_ref, src_ref, h2_hbm, wrow_ref, wg_ref, wu_ref, wd_ref,
                yd_ref, buf, sem):
    i = pl.program_id(0)
    nused = nused_ref[0]
    slot = i % 2

    def gather_copy(blk, dst_slot, r):
        tok = src_ref[blk * MOE_BLK + r]
        return pltpu.make_async_copy(h2_hbm.at[pl.ds(tok, 1)], buf.at[dst_slot, pl.ds(r, 1)],
                                     sem.at[dst_slot])

    def start_gather(blk, dst_slot):
        def body(r, carry):
            gather_copy(blk, dst_slot, r).start()
            return carry
        lax.fori_loop(0, MOE_BLK, body, 0)

    @pl.when((i == 0) & (nused > 0))
    def _prime():
        start_gather(0, 0)

    @pl.when(i + 1 < nused)
    def _prefetch():
        start_gather(i + 1, 1 - slot)

    @pl.when(i < nused)
    def _compute():
        pltpu.make_async_copy(h2_hbm.at[pl.ds(0, MOE_BLK)], buf.at[slot], sem.at[slot]).wait()
        xb = buf[slot].astype(BF16)
        gate = _dot(xb, wg_ref[0])
        up = _dot(xb, wu_ref[0])
        hid = (gate * _sigmoid(gate)) * up
        yd_ref[...] = _dot(hid.astype(BF16), wd_ref[0]) * wrow_ref[...]

    @pl.when(i >= nused)
    def _unused():
        yd_ref[...] = jnp.zeros_like(yd_ref)


def _moe(blk_e, nused, src_tok, h2_flat, w_row, wg, wu, wd):
    nb = blk_e.shape[0]
    d = h2_flat.shape[1]
    return pl.pallas_call(
        _moe_kernel,
        grid_spec=pltpu.PrefetchScalarGridSpec(
            num_scalar_prefetch=3,
            grid=(nb,),
            in_specs=[
                pl.BlockSpec(memory_space=pl.ANY),
                pl.BlockSpec((MOE_BLK, 1), lambda i, be, nu, st: (i, 0)),
                pl.BlockSpec((1, d, D_EXPERT), lambda i, be, nu, st: (be[i], 0, 0)),
                pl.BlockSpec((1, d, D_EXPERT), lambda i, be, nu, st: (be[i], 0, 0)),
                pl.BlockSpec((1, D_EXPERT, d), lambda i, be, nu, st: (be[i], 0, 0)),
            ],
            out_specs=pl.BlockSpec((MOE_BLK, d), lambda i, be, nu, st: (i, 0)),
            scratch_shapes=[pltpu.VMEM((2, MOE_BLK, d), F32), pltpu.SemaphoreType.DMA((2,))],
        ),
        out_shape=jax.ShapeDtypeStruct((nb * MOE_BLK, d), F32),
        compiler_params=pltpu.CompilerParams(
            dimension_semantics=("arbitrary",), vmem_limit_bytes=VMEM_LIMIT),
        name="moe_experts",
    )(blk_e, nused, src_tok, h2_flat, w_row, wg, wu, wd)


def _combine_kernel(dest_ref, x_ref, mod_ref, g_ref, yd_hbm, o_ref, buf, sem, *, final):
    s = pl.program_id(0)
    n_steps = pl.num_programs(0)
    tm = x_ref.shape[0]
    slot = s % 2

    def start_gather(step, dst_slot):
        def body(r, carry):
            for k in range(2):
                row = dest_ref[(step * tm + r) * 2 + k]
                pltpu.make_async_copy(yd_hbm.at[pl.ds(row, 1)], buf.at[dst_slot, k, pl.ds(r, 1)],
                                      sem.at[dst_slot]).start()
            return carry
        lax.fori_loop(0, tm, body, 0)

    @pl.when(s == 0)
    def _prime():
        start_gather(0, 0)

    @pl.when(s + 1 < n_steps)
    def _prefetch():
        start_gather(s + 1, 1 - slot)

    for k in range(2):
        pltpu.make_async_copy(yd_hbm.at[pl.ds(0, tm)], buf.at[slot, k], sem.at[slot]).wait()
    x_new = x_ref[...] + mod_ref[0, 5:6, :] * (buf[slot, 0] + buf[slot, 1])
    if final:
        ms = jnp.mean(x_new * x_new, axis=-1, keepdims=True)
        x_new = x_new * lax.rsqrt(ms + EPS) * g_ref[...]
    o_ref[...] = x_new


def _combine(dest, x_flat, mod8, g_final, yd, n_lat, n_tiles, final):
    ntok, d = x_flat.shape
    tm = TOK_TILE
    n_steps = ntok // tm
    bsz = n_steps // n_tiles

    def mod_map(s, dref):
        return (jnp.where(s % n_tiles < n_lat, s // n_tiles, bsz), 0, 0)

    return pl.pallas_call(
        functools.partial(_combine_kernel, final=final),
        grid_spec=pltpu.PrefetchScalarGridSpec(
            num_scalar_prefetch=1,
            grid=(n_steps,),
            in_specs=[
                pl.BlockSpec((tm, d), lambda s, dref: (s, 0)),
                pl.BlockSpec((1, 8, d), mod_map),
                pl.BlockSpec((1, d), lambda s, dref: (0, 0)),
                pl.BlockSpec(memory_space=pl.ANY),
            ],
            out_specs=pl.BlockSpec((tm, d), lambda s, dref: (s, 0)),
            scratch_shapes=[pltpu.VMEM((2, 2, tm, d), F32), pltpu.SemaphoreType.DMA((2,))],
        ),
        out_shape=jax.ShapeDtypeStruct((ntok, d), F32),
        compiler_params=pltpu.CompilerParams(
            dimension_semantics=("arbitrary",), vmem_limit_bytes=VMEM_LIMIT),
        name="combine",
    )(dest, x_flat, mod8, g_final, yd)


def _rope_swap_perm():
    half = HEAD_DIM // 2
    quarter = half // 2
    perm = np.arange(HEAD_DIM)
    for base in (0, half):
        perm[base:base + quarter] = np.arange(base + quarter, base + half)
        perm[base + quarter:base + half] = np.arange(base, base + quarter)
    return perm


def _prep_w_in(w_in):
    sizes = (D_CONV, D_CONV, D_CONV, D_NA, D_NA, D_NA, D_SW, D_KV, D_KV, 3 * D_MODEL)
    offs = np.concatenate([[0], np.cumsum(sizes)])
    a_h, a_b, a_c, nq, nk, nv, wq, wk, wv, gates = [w_in[:, offs[j]:offs[j + 1]] for j in range(len(sizes))]
    swap = _rope_swap_perm()
    q_cols = np.concatenate([h * HEAD_DIM + np.arange(HEAD_DIM) for h in _SW_HEAD_ORDER])
    q_cols_s = np.concatenate([h * HEAD_DIM + swap for h in _SW_HEAD_ORDER])
    k_cols_s = np.concatenate([h * HEAD_DIM + swap for h in range(H_KV)])
    w = jnp.concatenate([a_h, a_b, a_c, nq, nk, nv, wq[:, q_cols], wq[:, q_cols_s], wk, wk[:, k_cols_s],
                         wv, gates], axis=1)
    return w.astype(BF16)


def _rope_tables(seq, ctx_len):
    half = HEAD_DIM // 2
    quarter = half // 2
    t = jnp.arange(seq)
    row = (t // GRID_W).astype(F32)
    col = (t % GRID_W).astype(F32)
    inv = ROPE_BASE ** (-jnp.arange(quarter, dtype=F32) / quarter)
    ang_r = row[:, None] * inv[None, :]
    ang_c = col[:, None] * inv[None, :]
    cos = jnp.concatenate([jnp.cos(ang_r)] * 2 + [jnp.cos(ang_c)] * 2, axis=-1)
    sin = jnp.concatenate([-jnp.sin(ang_r), jnp.sin(ang_r), -jnp.sin(ang_c), jnp.sin(ang_c)], axis=-1)
    cos = jnp.concatenate([cos, jnp.ones((ctx_len, HEAD_DIM), F32)], axis=0)
    sin = jnp.concatenate([sin, jnp.zeros((ctx_len, HEAD_DIM), F32)], axis=0)
    return jnp.concatenate([cos, cos], axis=-1), jnp.concatenate([sin, sin], axis=-1)


def _na_bias_tables(rpb, rows):
    qc = np.arange(GRID_W)
    kc = np.arange(GRID_W)
    cstart = np.clip(qc - WIN_C // 2, 0, GRID_W - WIN_C)
    rel_c = np.clip(kc[None, :] - qc[:, None] + WIN_C - 1, 0, 2 * WIN_C - 2)
    col_ok = (kc[None, :] >= cstart[:, None]) & (kc[None, :] < cstart[:, None] + WIN_C)
    t = jnp.where(col_ok[None, None], rpb[:, :, rel_c], NEG_INF)
    masked = jnp.full((rpb.shape[0], GRID_W, GRID_W), NEG_INF, F32)
    n_blk = rows // NA_ROWS
    cases = []
    for blk in (0, min(1, n_blk - 1), n_blk - 1):
        r0 = blk * NA_ROWS
        ks = int(np.clip(r0 - WIN_R // 2, 0, rows - NA_KROWS))
        q_rows = []
        for qr in range(NA_ROWS):
            r = r0 + qr
            rs = int(np.clip(r - WIN_R // 2, 0, rows - WIN_R))
            tiles = []
            for a in range(NA_KROWS):
                kr = ks + a
                tiles.append(t[:, kr - r + WIN_R - 1] if rs <= kr < rs + WIN_R else masked)
            q_rows.append(jnp.concatenate(tiles, axis=-1))
        cases.append(jnp.concatenate(q_rows, axis=-2))
    return jnp.stack(cases, axis=0).astype(F32)


def _routing_plan(rt, n_blocks):
    ntok = rt.shape[0]
    eid = rt[:, 0:2].astype(jnp.int32).reshape(-1)
    wts = rt[:, 2:4].reshape(-1)
    onehot = (eid[:, None] == jnp.arange(N_EXPERTS)[None, :]).astype(jnp.int32)
    csum = jnp.cumsum(onehot, axis=0)
    counts = csum[-1]
    rank = jnp.take_along_axis(csum, eid[:, None], axis=1)[:, 0] - 1
    padded = ((counts + MOE_BLK - 1) // MOE_BLK) * MOE_BLK
    pad_end = jnp.cumsum(padded)
    pad_start = pad_end - padded
    dest = (pad_start[eid] + rank).astype(jnp.int32)
    tok = jnp.arange(ntok * 2, dtype=jnp.int32) // 2
    src_tok = jnp.zeros((n_blocks * MOE_BLK,), jnp.int32).at[dest].set(tok)
    w_row = jnp.zeros((n_blocks * MOE_BLK,), F32).at[dest].set(wts)
    blk_start = jnp.arange(n_blocks, dtype=jnp.int32) * MOE_BLK
    blk_e = jnp.clip(jnp.sum(pad_end[None, :] <= blk_start[:, None], axis=1), 0, N_EXPERTS - 1).astype(jnp.int32)
    nused = (pad_end[-1] // MOE_BLK).astype(jnp.int32).reshape(1)
    return blk_e, nused, src_tok, w_row[:, None], dest


def kernel(x, c, ctx, c_ctx, w_mod, b_mod, g_attn, g_ffn, w_in, conv_w, rpb, sinks, w_branch, w_o,
           w_rg, b_rg, w_re, b_re, w_e_gate, w_e_up, w_e_down, g_final):
    bsz, seq, d = x.shape
    ctx_len = ctx.shape[1]
    depth = w_in.shape[0]
    ltot = seq + ctx_len
    n_lat = seq // TOK_TILE
    n_tiles = ltot // TOK_TILE
    rows = seq // GRID_W
    ntok = bsz * ltot
    n_assign = ntok * 2
    n_blocks = (n_assign + N_EXPERTS * (MOE_BLK - 1) + MOE_BLK - 1) // MOE_BLK

    x_all = jnp.concatenate([x, ctx], axis=1)
    cos_t, sin_t = _rope_tables(seq, ctx_len)
    assert bsz + 1 <= 8
    c8 = jnp.pad(jnp.concatenate([c, c_ctx[None, :]], axis=0), ((0, 7 - bsz), (0, 0)))
    mod_all = _mod_vectors(c8, w_mod, b_mod)
    yb_rows = np.concatenate([h * HEAD_DIM + np.arange(HEAD_DIM) for h in _SW_HEAD_ORDER])

    for l in range(depth):
        mod8 = jnp.pad(mod_all[l, :bsz + 1].reshape(bsz + 1, 6, d), ((0, 0), (0, 2), (0, 0)))
        w_l = _prep_w_in(w_in[l])
        bias = _na_bias_tables(rpb[l], rows)
        o2 = D_CONV + D_NA
        wb = jnp.concatenate([w_branch[l, :o2], w_branch[l, o2:][yb_rows]], axis=0).astype(BF16)
        wr = jnp.zeros((d, LANES), F32).at[:, 0:N_GROUPS].set(w_rg[l]).at[:, 8:8 + N_EXPERTS].set(w_re[l])
        wr_hi = wr.astype(BF16)
        wr_lo = (wr - wr_hi.astype(F32)).astype(BF16)
        br = jnp.zeros((1, LANES), F32).at[0, 0:N_GROUPS].set(b_rg[l]).at[0, 8:8 + N_EXPERTS].set(b_re[l])
        convw8 = jnp.pad(conv_w[l], ((0, 8 - CONV_W), (0, 0)))

        z, ab, nq, nk, nv, wq, wk, wv, gates = _inproj(
            x_all, mod8, g_attn[l][None, :], w_l, cos_t, sin_t, n_lat)
        y_b = _na_attn(nq, nk, nv, bias, seq)
        y_c = _sw_attn(sinks[l], wq, wk, wv, seq)
        x_mid, h2, rt = _merge(x_all, mod8, g_ffn[l][None, :], z, ab, y_b, y_c, gates, convw8, wb,
                               w_o[l].astype(BF16), wr_hi, wr_lo, br, n_lat)
        blk_e, nused, src_tok, w_row, dest = _routing_plan(rt.reshape(ntok, LANES), n_blocks)
        yd = _moe(blk_e, nused, src_tok, h2.reshape(ntok, d), w_row,
                  w_e_gate[l].astype(BF16), w_e_up[l].astype(BF16), w_e_down[l].astype(BF16))
        final = l == depth - 1
        x_flat = _combine(dest, x_mid.reshape(ntok, d), mod8, g_final[None, :], yd, n_lat, n_tiles, final)
        x_all = x_flat.reshape(bsz, ltot, d)
    return x_all[:, :seq]
```

```python
import functools

import numpy as np
import jax
import jax.numpy as jnp
from jax import lax
from jax.experimental import pallas as pl
from jax.experimental.pallas import tpu as pltpu

F32 = jnp.float32
BF16 = jnp.bfloat16

D_MODEL = 1024
GRID_W = 64
HEAD_DIM = 64
D_CONV = 256
CONV_W = 3
H_NA = 6
WIN_R = 8
WIN_C = 16
H_Q = 6
H_KV = 2
WINDOW = 128
ROPE_BASE = 10000.0
D_NA = H_NA * HEAD_DIM
D_SW = H_Q * HEAD_DIM
D_KV = H_KV * HEAD_DIM
N_GROUPS = 4
EXP_PER_GROUP = 8
N_EXPERTS = N_GROUPS * EXP_PER_GROUP
D_EXPERT = 256
MOE_BLK = 256
EPS = 1e-6
NEG_INF = -1e30

LANES = 128
ROW_SUB = D_MODEL // LANES
TOK_TILE = 256
NA_ROWS = TOK_TILE // GRID_W
NA_KROWS = NA_ROWS + WIN_R - 1
NA_KEYS = NA_KROWS * GRID_W
SW_KEYS = TOK_TILE + 2 * WINDOW
VMEM_LIMIT = 56 * 1024 * 1024
DMA_UNROLL = 8

_C_CONV = 0
_C_NQ = _C_CONV + 3 * D_CONV
_C_NK = _C_NQ + D_NA
_C_NV = _C_NK + D_NA
_C_WQ = _C_NV + D_NA
_C_WQS = _C_WQ + D_SW
_C_WK = _C_WQS + D_SW
_C_WKS = _C_WK + D_KV
_C_WV = _C_WKS + D_KV
_C_GATE = _C_WV + D_KV
_C_END = _C_GATE + 3 * D_MODEL

_SW_HEAD_ORDER = (0, 3, 1, 4, 2, 5)


def _dot(a, b):
    return jnp.dot(a, b, preferred_element_type=F32)


def _dot_nt(a, b):
    return lax.dot_general(a, b, (((1,), (1,)), ((), ())), preferred_element_type=F32)


def _sigmoid(x):
    return 1.0 / (1.0 + jnp.exp(-x))


def _modln(x, g, shift, scale):
    ms = jnp.mean(x * x, axis=-1, keepdims=True)
    return (x * lax.rsqrt(ms + EPS)) * g * (1.0 + scale) + shift


def _store_row_tiles(ref, val):
    n = val.shape[0]
    for s in range(ROW_SUB):
        ref[pl.ds(s, n, stride=ROW_SUB), :] = val[:, s * LANES:(s + 1) * LANES]


def _load_row_tiles(ref, n):
    return jnp.concatenate([ref[pl.ds(s, n, stride=ROW_SUB), :] for s in range(ROW_SUB)], axis=-1)


def _mod_kernel(c_ref, w_ref, b_ref, o_ref):
    cv = c_ref[...]
    sc = (cv * _sigmoid(cv)).astype(BF16)
    o_ref[0] = _dot(sc, w_ref[0].astype(BF16)) + b_ref[0]


def _mod_vectors(c8, w_mod, b_mod):
    depth, d, cols = w_mod.shape
    tn = cols // 4
    return pl.pallas_call(
        _mod_kernel,
        grid=(depth, cols // tn),
        in_specs=[
            pl.BlockSpec((8, d), lambda l, j: (0, 0)),
            pl.BlockSpec((1, d, tn), lambda l, j: (l, 0, j)),
            pl.BlockSpec((1, 1, tn), lambda l, j: (l, 0, j)),
        ],
        out_specs=pl.BlockSpec((1, 8, tn), lambda l, j: (l, 0, j)),
        out_shape=jax.ShapeDtypeStruct((depth, 8, cols), F32),
        compiler_params=pltpu.CompilerParams(
            dimension_semantics=("arbitrary", "arbitrary"), vmem_limit_bytes=VMEM_LIMIT),
        name="mod_vectors",
    )(c8, w_mod, b_mod.reshape(depth, 1, cols))


def _inproj_kernel(x_ref, mod_ref, g_ref, w_ref, cos_ref, sin_ref,
                   z_ref, ab_ref, nq_ref, nk_ref, nv_ref, wq_ref, wk_ref, wv_ref, gate_ref):
    x = x_ref[0]
    h = _modln(x, g_ref[...], mod_ref[0, 0:1, :], mod_ref[0, 1:2, :]).astype(BF16)

    def proj(c0, c1):
        return _dot(h, w_ref[:, c0:c1])

    a_h = proj(_C_CONV, _C_CONV + D_CONV)
    a_b = proj(_C_CONV + D_CONV, _C_CONV + 2 * D_CONV)
    a_c = proj(_C_CONV + 2 * D_CONV, _C_NQ)
    z_ref[0] = (a_c * a_h).astype(BF16)
    ab_ref[0] = a_b.astype(BF16)

    qk_scale = HEAD_DIM ** -0.5
    nq_ref[0] = (proj(_C_NQ, _C_NK) * qk_scale).astype(BF16)
    nk_ref[0] = proj(_C_NK, _C_NV).astype(BF16)
    nv_ref[0] = proj(_C_NV, _C_WQ).astype(BF16)

    cos = cos_ref[...]
    sin = sin_ref[...]
    cos3 = jnp.concatenate([cos] * (D_SW // LANES), axis=-1)
    sin3 = jnp.concatenate([sin] * (D_SW // LANES), axis=-1)
    wq = proj(_C_WQ, _C_WQS) * cos3 + proj(_C_WQS, _C_WK) * sin3
    wq_ref[0] = (wq * qk_scale).astype(BF16)
    wk_ref[0] = (proj(_C_WK, _C_WKS) * cos + proj(_C_WKS, _C_WV) * sin).astype(BF16)
    wv_ref[0] = proj(_C_WV, _C_GATE).astype(BF16)

    for j in range(3):
        c0 = _C_GATE + j * D_MODEL
        gate_ref[0, :, j * D_MODEL:(j + 1) * D_MODEL] = _sigmoid(proj(c0, c0 + D_MODEL)).astype(BF16)


def _inproj(x_all, mod8, g, w, cos_t, sin_t, n_lat):
    bsz, ltot, d = x_all.shape
    n_tiles = ltot // TOK_TILE
    tm = TOK_TILE

    def tok_spec(width):
        return pl.BlockSpec((1, tm, width), lambda b, i: (b, i, 0))

    def mod_map(b, i):
        return (jnp.where(i < n_lat, b, bsz), 0, 0)

    widths = (D_CONV, D_CONV, D_NA, D_NA, D_NA, D_SW, D_KV, D_KV, 3 * D_MODEL)
    return pl.pallas_call(
        _inproj_kernel,
        grid=(bsz, n_tiles),
        in_specs=[
            tok_spec(d),
            pl.BlockSpec((1, 8, d), mod_map),
            pl.BlockSpec((1, d), lambda b, i: (0, 0)),
            pl.BlockSpec((d, _C_END), lambda b, i: (0, 0), pipeline_mode=pl.Buffered(1)),
            pl.BlockSpec((tm, LANES), lambda b, i: (i, 0)),
            pl.BlockSpec((tm, LANES), lambda b, i: (i, 0)),
        ],
        out_specs=[tok_spec(wd) for wd in widths],
        out_shape=[jax.ShapeDtypeStruct((bsz, ltot, wd), BF16) for wd in widths],
        compiler_params=pltpu.CompilerParams(
            dimension_semantics=("arbitrary", "arbitrary"), vmem_limit_bytes=VMEM_LIMIT),
        name="inproj",
    )(x_all, mod8, g, w, cos_t, sin_t)


def _half_mask(half):
    lane = lax.broadcasted_iota(jnp.int32, (1, LANES), 1)
    return (lane < HEAD_DIM) if half == 0 else (lane >= HEAD_DIM)


def _softmax_pv(parts, sink=None):
    m = parts[0][0].max(axis=-1, keepdims=True)
    for s, _ in parts[1:]:
        m = jnp.maximum(m, s.max(axis=-1, keepdims=True))
    if sink is not None:
        m = jnp.maximum(m, sink)
    l = None
    o = None
    for s, v in parts:
        e = jnp.exp(s - m)
        ls = e.sum(axis=-1, keepdims=True)
        os_ = _dot(e.astype(BF16), v)
        l = ls if l is None else l + ls
        o = os_ if o is None else o + os_
    if sink is not None:
        l = l + jnp.exp(sink - m)
    return o * (1.0 / l)


def _na_kernel(q_ref, k_ref, v_ref, bias_ref, o_ref, *, n_blk, rows, seq):
    i = pl.program_id(1)
    ctx_len = k_ref.shape[1] - seq

    def pair_slices(p):
        return slice(p * LANES, (p + 1) * LANES)

    @pl.when(i < n_blk)
    def _latent():
        r0 = i * NA_ROWS
        ks = jnp.clip(r0 - WIN_R // 2, 0, rows - NA_KROWS)
        kstart = pl.multiple_of(ks * GRID_W, GRID_W)
        for p in range(H_NA // 2):
            ps = pair_slices(p)
            q2 = q_ref[0, :, ps]
            k2 = k_ref[0, pl.ds(kstart, NA_KEYS), ps]
            v2 = v_ref[0, pl.ds(kstart, NA_KEYS), ps]
            kc2 = k_ref[0, seq:seq + ctx_len, ps]
            vc2 = v_ref[0, seq:seq + ctx_len, ps]
            outs = []
            for half in range(2):
                qm = jnp.where(_half_mask(half), q2, jnp.zeros_like(q2))
                s = _dot_nt(qm, k2) + bias_ref[0, 2 * p + half]
                sc = _dot_nt(qm, kc2)
                outs.append(_softmax_pv([(s, v2), (sc, vc2)]))
            o_ref[0, :, ps] = jnp.where(_half_mask(0), outs[0], outs[1]).astype(BF16)

    @pl.when(i == n_blk)
    def _context():
        for p in range(H_NA // 2):
            ps = pair_slices(p)
            q2 = q_ref[0, :, ps]
            kc2 = k_ref[0, seq:seq + ctx_len, ps]
            vc2 = v_ref[0, seq:seq + ctx_len, ps]
            outs = []
            for half in range(2):
                qm = jnp.where(_half_mask(half), q2, jnp.zeros_like(q2))
                outs.append(_softmax_pv([(_dot_nt(qm, kc2), vc2)]))
            o_ref[0, :, ps] = jnp.where(_half_mask(0), outs[0], outs[1]).astype(BF16)


def _na_attn(nq, nk, nv, bias, seq):
    bsz, ltot, _ = nq.shape
    rows = seq // GRID_W
    n_blk = seq // TOK_TILE
    n_steps = ltot // TOK_TILE

    def bias_map(b, i):
        return (jnp.where(i == 0, 0, jnp.where(i >= n_blk - 1, 2, 1)), 0, 0, 0)

    return pl.pallas_call(
        functools.partial(_na_kernel, n_blk=n_blk, rows=rows, seq=seq),
        grid=(bsz, n_steps),
        in_specs=[
            pl.BlockSpec((1, TOK_TILE, D_NA), lambda b, i: (b, i, 0)),
            pl.BlockSpec((1, ltot, D_NA), lambda b, i: (b, 0, 0)),
            pl.BlockSpec((1, ltot, D_NA), lambda b, i: (b, 0, 0)),
            pl.BlockSpec((1, H_NA, TOK_TILE, NA_KEYS), bias_map),
        ],
        out_specs=pl.BlockSpec((1, TOK_TILE, D_NA), lambda b, i: (b, i, 0)),
        out_shape=jax.ShapeDtypeStruct((bsz, ltot, D_NA), BF16),
        compiler_params=pltpu.CompilerParams(
            dimension_semantics=("arbitrary", "arbitrary"), vmem_limit_bytes=VMEM_LIMIT),
        name="na_attn",
    )(nq, nk, nv, bias)


def _sw_kernel(sink_ref, q_ref, k_ref, v_ref, o_ref, *, n_blk, seq):
    i = pl.program_id(1)
    ctx_len = k_ref.shape[1] - seq
    kc2 = k_ref[0, seq:seq + ctx_len, :]
    vc2 = v_ref[0, seq:seq + ctx_len, :]

    def heads_of_pair(p, window_parts):
        q2 = q_ref[0, :, p * LANES:(p + 1) * LANES]
        outs = []
        for half in range(2):
            sink = sink_ref[half * (H_Q // H_KV) + p]
            qm = jnp.where(_half_mask(half), q2, jnp.zeros_like(q2))
            parts = [(fn(qm), v) for fn, v in window_parts] + [(_dot_nt(qm, kc2), vc2)]
            outs.append(_softmax_pv(parts, sink=sink))
        o_ref[0, :, p * LANES:(p + 1) * LANES] = jnp.where(_half_mask(0), outs[0], outs[1]).astype(BF16)

    @pl.when(i < n_blk)
    def _latent():
        t0 = i * TOK_TILE
        kst = pl.multiple_of(jnp.clip(t0 - WINDOW, 0, seq - SW_KEYS), WINDOW)
        qpos = t0 + lax.broadcasted_iota(jnp.int32, (TOK_TILE, 1), 0)
        kpos = kst + lax.broadcasted_iota(jnp.int32, (1, SW_KEYS), 1)
        valid = jnp.abs(qpos - kpos) <= WINDOW
        k2 = k_ref[0, pl.ds(kst, SW_KEYS), :]
        v2 = v_ref[0, pl.ds(kst, SW_KEYS), :]
        for p in range(H_Q // 2):
            heads_of_pair(p, [(lambda qm: jnp.where(valid, _dot_nt(qm, k2), NEG_INF), v2)])

    @pl.when(i == n_blk)
    def _context():
        for p in range(H_Q // 2):
            heads_of_pair(p, [])


def _sw_attn(sinks, wq, wk, wv, seq):
    bsz, ltot, _ = wq.shape
    n_blk = seq // TOK_TILE
    n_steps = ltot // TOK_TILE
    return pl.pallas_call(
        functools.partial(_sw_kernel, n_blk=n_blk, seq=seq),
        grid_spec=pltpu.PrefetchScalarGridSpec(
            num_scalar_prefetch=1,
            grid=(bsz, n_steps),
            in_specs=[
                pl.BlockSpec((1, TOK_TILE, D_SW), lambda b, i, s: (b, i, 0)),
                pl.BlockSpec((1, ltot, D_KV), lambda b, i, s: (b, 0, 0)),
                pl.BlockSpec((1, ltot, D_KV), lambda b, i, s: (b, 0, 0)),
            ],
            out_specs=pl.BlockSpec((1, TOK_TILE, D_SW), lambda b, i, s: (b, i, 0)),
        ),
        out_shape=jax.ShapeDtypeStruct((bsz, ltot, D_SW), BF16),
        compiler_params=pltpu.CompilerParams(
            dimension_semantics=("arbitrary", "arbitrary"), vmem_limit_bytes=VMEM_LIMIT),
        name="sw_attn",
    )(sinks, wq, wk, wv)


def _split_bf16(x):
    hi = x.astype(BF16)
    lo = (x - hi.astype(F32)).astype(BF16)
    return hi, lo


def _merge_kernel(x_ref, mod_ref, g_ref, z_ref, zp_ref, zn_ref, ab_ref, yb_ref, yc_ref, gate_ref,
                  convw_ref, wb_ref, wo_ref, wr_hi_ref, wr_lo_ref, br_ref,
                  xo_ref, h2_ref, rt_ref, *, n_lat):
    i = pl.program_id(1)
    n_tiles = pl.num_programs(1)
    tm = x_ref.shape[1]

    z = z_ref[0].astype(F32)
    halo = zp_ref.shape[1]
    first = (i == 0) | (i == n_lat)
    last = (i == n_lat - 1) | (i == n_tiles - 1)
    prev_row = jnp.where(first, 0.0, zp_ref[0, halo - 1:halo, :].astype(F32))
    next_row = jnp.where(last, 0.0, zn_ref[0, 0:1, :].astype(F32))
    row = lax.broadcasted_iota(jnp.int32, (tm, 1), 0)
    z_prev = jnp.where(row == 0, prev_row, pltpu.roll(z, 1, axis=0))
    z_next = jnp.where(row == tm - 1, next_row, pltpu.roll(z, tm - 1, axis=0))
    cw = convw_ref[...]
    y_a = ab_ref[0].astype(F32) * (z_prev * cw[0:1] + z * cw[1:2] + z_next * cw[2:3])

    o1 = D_CONV
    o2 = D_CONV + D_NA
    g_a = gate_ref[0, :, 0:D_MODEL].astype(F32)
    g_b = gate_ref[0, :, D_MODEL:2 * D_MODEL].astype(F32)
    g_c = gate_ref[0, :, 2 * D_MODEL:3 * D_MODEL].astype(F32)
    m = (g_a * _dot(y_a.astype(BF16), wb_ref[0:o1, :])
         + g_b * _dot(yb_ref[0], wb_ref[o1:o2, :])
         + g_c * _dot(yc_ref[0], wb_ref[o2:, :]))
    y = _dot(m.astype(BF16), wo_ref[...])
    x_new = x_ref[0] + mod_ref[0, 2:3, :] * y
    xo_ref[0] = x_new

    h2 = _modln(x_new, g_ref[...], mod_ref[0, 3:4, :], mod_ref[0, 4:5, :])
    _store_row_tiles(h2_ref, h2)

    h_hi, h_lo = _split_bf16(h2)
    lg = (_dot(h_hi, wr_hi_ref[...]) + _dot(h_hi, wr_lo_ref[...]) + _dot(h_lo, wr_hi_ref[...])
          + br_ref[...])
    lane = lax.broadcasted_iota(jnp.int32, lg.shape, 1)
    big = jnp.int32(LANES)

    def masked_argmax(vals, mask):
        v = jnp.where(mask, vals, -jnp.inf)
        vmax = v.max(axis=-1, keepdims=True)
        idx = jnp.where(mask & (v == vmax), lane, big).min(axis=-1, keepdims=True)
        return vmax, idx

    is_grp = lane < N_GROUPS
    g_max, g_idx = masked_argmax(lg, is_grp)
    pg_top = 1.0 / jnp.where(is_grp, jnp.exp(lg - g_max), 0.0).sum(axis=-1, keepdims=True)
    e_lo = 8 + g_idx * EXP_PER_GROUP
    in_grp = (lane >= e_lo) & (lane < e_lo + EXP_PER_GROUP)
    v1, i1 = masked_argmax(lg, in_grp)
    v2, i2 = masked_argmax(lg, in_grp & (lane != i1))
    e21 = jnp.exp(v2 - v1)
    w1 = pg_top / (1.0 + e21)
    w2 = pg_top * e21 / (1.0 + e21)
    rt = jnp.where(lane == 0, (i1 - 8).astype(F32),
                   jnp.where(lane == 1, (i2 - 8).astype(F32),
                             jnp.where(lane == 2, w1, jnp.where(lane == 3, w2, 0.0))))
    rt_ref[0] = rt


def _merge(x_all, mod8, g, z, ab, y_b, y_c, gates, conv_w, wb, wo, wr_hi, wr_lo, br, n_lat):
    bsz, ltot, d = x_all.shape
    n_tiles = ltot // TOK_TILE
    tm = TOK_TILE
    halo = 16
    hb = tm // halo
    n_halo = ltot // halo

    def tok_spec(width):
        return pl.BlockSpec((1, tm, width), lambda b, i: (b, i, 0))

    def mod_map(b, i):
        return (jnp.where(i < n_lat, b, bsz), 0, 0)

    def const2(shape):
        return pl.BlockSpec(shape, lambda b, i: (0, 0))

    return pl.pallas_call(
        functools.partial(_merge_kernel, n_lat=n_lat),
        grid=(bsz, n_tiles),
        in_specs=[
            tok_spec(d),
            pl.BlockSpec((1, 8, d), mod_map),
            const2((1, d)),
            tok_spec(D_CONV),
            pl.BlockSpec((1, halo, D_CONV), lambda b, i: (b, jnp.maximum(i * hb - 1, 0), 0)),
            pl.BlockSpec((1, halo, D_CONV), lambda b, i: (b, jnp.minimum((i + 1) * hb, n_halo - 1), 0)),
            tok_spec(D_CONV),
            tok_spec(D_NA),
            tok_spec(D_SW),
            tok_spec(3 * d),
            const2((8, D_CONV)),
            const2((d, d)),
            const2((d, d)),
            const2((d, LANES)),
            const2((d, LANES)),
            const2((1, LANES)),
        ],
        out_specs=[tok_spec(d),
                   pl.BlockSpec((tm * ROW_SUB, LANES), lambda b, i: (b * n_tiles + i, 0)),
                   tok_spec(LANES)],
        out_shape=[jax.ShapeDtypeStruct((bsz, ltot, d), F32),
                   jax.ShapeDtypeStruct((bsz * ltot * ROW_SUB, LANES), F32),
                   jax.ShapeDtypeStruct((bsz, ltot, LANES), F32)],
        compiler_params=pltpu.CompilerParams(
            dimension_semantics=("arbitrary", "arbitrary"), vmem_limit_bytes=VMEM_LIMIT),
        name="merge",
    )(x_all, mod8, g, z, z, z, ab, y_b, y_c, gates, conv_w, wb, wo, wr_hi, wr_lo, br)


def _dispatch_kernel(dest_ref, h2_ref, xd_in, xd_hbm, sem):
    del xd_in
    s = pl.program_id(0)
    tm = h2_ref.shape[0] // ROW_SUB

    def row_copy(r, k):
        dst = pl.multiple_of(dest_ref[(s * tm + r) * 2 + k], ROW_SUB)
        src = pl.multiple_of(r * ROW_SUB, ROW_SUB)
        return pltpu.make_async_copy(h2_ref.at[pl.ds(src, ROW_SUB)], xd_hbm.at[pl.ds(dst, ROW_SUB)], sem)

    def body(r, carry):
        for k in range(2):
            row_copy(r, k).start()
        return carry
    lax.fori_loop(0, tm, body, 0, unroll=DMA_UNROLL)

    for k in range(2):
        pltpu.make_async_copy(h2_ref, xd_hbm.at[pl.ds(0, tm * ROW_SUB)], sem).wait()


def _dispatch(dest8, h2t, n_blocks):
    tm = TOK_TILE
    n_steps = h2t.shape[0] // (tm * ROW_SUB)
    xd_rows = n_blocks * MOE_BLK * ROW_SUB
    return pl.pallas_call(
        _dispatch_kernel,
        grid_spec=pltpu.PrefetchScalarGridSpec(
            num_scalar_prefetch=1,
            grid=(n_steps,),
            in_specs=[
                pl.BlockSpec((tm * ROW_SUB, LANES), lambda s, dref: (s, 0)),
                pl.BlockSpec(memory_space=pl.ANY),
            ],
            out_specs=pl.BlockSpec(memory_space=pl.ANY),
            scratch_shapes=[pltpu.SemaphoreType.DMA(())],
        ),
        out_shape=jax.ShapeDtypeStruct((xd_rows, LANES), F32),
        input_output_aliases={2: 0},
        compiler_params=pltpu.CompilerParams(
            dimension_semantics=("arbitrary",), vmem_limit_bytes=VMEM_LIMIT),
        name="moe_dispatch",
    )(dest8, h2t, jnp.zeros((xd_rows, LANES), F32))


def _moe_kernel(blk_e_ref, nused_ref, xd_ref, wg_ref, wu_ref, wd_ref, yd_ref):
    i = pl.program_id(0)

    @pl.when(i < nused_ref[0])
    def _compute():
        xb = _load_row_tiles(xd_ref, MOE_BLK).astype(BF16)
        gate = _dot(xb, wg_ref[0])
        up = _dot(xb, wu_ref[0])
        hid = (gate * _sigmoid(gate)) * up
        _store_row_tiles(yd_ref, _dot(hid.astype(BF16), wd_ref[0]))

    @pl.when(i >= nused_ref[0])
    def _unused():
        yd_ref[...] = jnp.zeros_like(yd_ref)


def _moe(blk_e, nused, xd, wg, wu, wd):
    nb = blk_e.shape[0]
    d = wg.shape[1]
    blk_rows = MOE_BLK * ROW_SUB

    def x_map(i, be, nu):
        return (jnp.minimum(i, jnp.maximum(nu[0] - 1, 0)), 0)

    return pl.pallas_call(
        _moe_kernel,
        grid_spec=pltpu.PrefetchScalarGridSpec(
            num_scalar_prefetch=2,
            grid=(nb,),
            in_specs=[
                pl.BlockSpec((blk_rows, LANES), x_map),
                pl.BlockSpec((1, d, D_EXPERT), lambda i, be, nu: (be[i], 0, 0)),
                pl.BlockSpec((1, d, D_EXPERT), lambda i, be, nu: (be[i], 0, 0)),
                pl.BlockSpec((1, D_EXPERT, d), lambda i, be, nu: (be[i], 0, 0)),
            ],
            out_specs=pl.BlockSpec((blk_rows, LANES), lambda i, be, nu: (i, 0)),
        ),
        out_shape=jax.ShapeDtypeStruct((nb * blk_rows, LANES), F32),
        compiler_params=pltpu.CompilerParams(
            dimension_semantics=("arbitrary",), vmem_limit_bytes=VMEM_LIMIT),
        name="moe_experts",
    )(blk_e, nused, xd, wg, wu, wd)


def _combine_kernel(dest_ref, x_ref, mod_ref, g_ref, rt_ref, yd_hbm, o_ref, buf, sem, *, final):
    s = pl.program_id(0)
    n_steps = pl.num_programs(0)
    tm = x_ref.shape[0]
    slot = s % 2

    def start_gather(step, dst_slot):
        def body(r, carry):
            for k in range(2):
                src = pl.multiple_of(dest_ref[(step * tm + r) * 2 + k], ROW_SUB)
                dst = pl.multiple_of(r * ROW_SUB, ROW_SUB)
                pltpu.make_async_copy(yd_hbm.at[pl.ds(src, ROW_SUB)], buf.at[dst_slot, k, pl.ds(dst, ROW_SUB)],
                                      sem.at[dst_slot]).start()
            return carry
        lax.fori_loop(0, tm, body, 0, unroll=DMA_UNROLL)

    @pl.when(s == 0)
    def _prime():
        start_gather(0, 0)

    @pl.when(s + 1 < n_steps)
    def _prefetch():
        start_gather(s + 1, 1 - slot)

    for k in range(2):
        pltpu.make_async_copy(yd_hbm.at[pl.ds(0, tm * ROW_SUB)], buf.at[slot, k], sem.at[slot]).wait()
    w0 = rt_ref[:, 2:3]
    w1 = rt_ref[:, 3:4]
    y = w0 * _load_row_tiles(buf.at[slot, 0], tm) + w1 * _load_row_tiles(buf.at[slot, 1], tm)
    x_new = x_ref[...] + mod_ref[0, 5:6, :] * y
    if final:
        ms = jnp.mean(x_new * x_new, axis=-1, keepdims=True)
        x_new = x_new * lax.rsqrt(ms + EPS) * g_ref[...]
    o_ref[...] = x_new


def _combine(dest, x_flat, mod8, g_final, rt, yd, n_lat, n_tiles, final):
    ntok, d = x_flat.shape
    tm = TOK_TILE
    n_steps = ntok // tm
    bsz = n_steps // n_tiles

    def mod_map(s, dref):
        return (jnp.where(s % n_tiles < n_lat, s // n_tiles, bsz), 0, 0)

    return pl.pallas_call(
        functools.partial(_combine_kernel, final=final),
        grid_spec=pltpu.PrefetchScalarGridSpec(
            num_scalar_prefetch=1,
            grid=(n_steps,),
            in_specs=[
                pl.BlockSpec((tm, d), lambda s, dref: (s, 0)),
                pl.BlockSpec((1, 8, d), mod_map),
                pl.BlockSpec((1, d), lambda s, dref: (0, 0)),
                pl.BlockSpec((tm, LANES), lambda s, dref: (s, 0)),
                pl.BlockSpec(memory_space=pl.ANY),
            ],
            out_specs=pl.BlockSpec((tm, d), lambda s, dref: (s, 0)),
            scratch_shapes=[pltpu.VMEM((2, 2, tm * ROW_SUB, LANES), F32), pltpu.SemaphoreType.DMA((2,))],
        ),
        out_shape=jax.ShapeDtypeStruct((ntok, d), F32),
        compiler_params=pltpu.CompilerParams(
            dimension_semantics=("arbitrary",), vmem_limit_bytes=VMEM_LIMIT),
        name="combine",
    )(dest, x_flat, mod8, g_final, rt, yd)


def _rope_swap_perm():
    half = HEAD_DIM // 2
    quarter = half // 2
    perm = np.arange(HEAD_DIM)
    for base in (0, half):
        perm[base:base + quarter] = np.arange(base + quarter, base + half)
        perm[base + quarter:base + half] = np.arange(base, base + quarter)
    return perm


def _prep_w_in(w_in):
    sizes = (D_CONV, D_CONV, D_CONV, D_NA, D_NA, D_NA, D_SW, D_KV, D_KV, 3 * D_MODEL)
    offs = np.concatenate([[0], np.cumsum(sizes)])
    a_h, a_b, a_c, nq, nk, nv, wq, wk, wv, gates = [w_in[:, offs[j]:offs[j + 1]] for j in range(len(sizes))]
    swap = _rope_swap_perm()
    q_cols = np.concatenate([h * HEAD_DIM + np.arange(HEAD_DIM) for h in _SW_HEAD_ORDER])
    q_cols_s = np.concatenate([h * HEAD_DIM + swap for h in _SW_HEAD_ORDER])
    k_cols_s = np.concatenate([h * HEAD_DIM + swap for h in range(H_KV)])
    w = jnp.concatenate([a_h, a_b, a_c, nq, nk, nv, wq[:, q_cols], wq[:, q_cols_s], wk, wk[:, k_cols_s],
                         wv, gates], axis=1)
    return w.astype(BF16)


def _rope_tables(seq, ctx_len):
    half = HEAD_DIM // 2
    quarter = half // 2
    t = jnp.arange(seq)
    row = (t // GRID_W).astype(F32)
    col = (t % GRID_W).astype(F32)
    inv = ROPE_BASE ** (-jnp.arange(quarter, dtype=F32) / quarter)
    ang_r = row[:, None] * inv[None, :]
    ang_c = col[:, None] * inv[None, :]
    cos = jnp.concatenate([jnp.cos(ang_r)] * 2 + [jnp.cos(ang_c)] * 2, axis=-1)
    sin = jnp.concatenate([-jnp.sin(ang_r), jnp.sin(ang_r), -jnp.sin(ang_c), jnp.sin(ang_c)], axis=-1)
    cos = jnp.concatenate([cos, jnp.ones((ctx_len, HEAD_DIM), F32)], axis=0)
    sin = jnp.concatenate([sin, jnp.zeros((ctx_len, HEAD_DIM), F32)], axis=0)
    return jnp.concatenate([cos, cos], axis=-1), jnp.concatenate([sin, sin], axis=-1)


def _na_bias_tables(rpb, rows):
    qc = np.arange(GRID_W)
    kc = np.arange(GRID_W)
    cstart = np.clip(qc - WIN_C // 2, 0, GRID_W - WIN_C)
    rel_c = np.clip(kc[None, :] - qc[:, None] + WIN_C - 1, 0, 2 * WIN_C - 2)
    col_ok = (kc[None, :] >= cstart[:, None]) & (kc[None, :] < cstart[:, None] + WIN_C)
    t = jnp.where(col_ok[None, None], rpb[:, :, rel_c], NEG_INF)
    masked = jnp.full((rpb.shape[0], GRID_W, GRID_W), NEG_INF, F32)
    n_blk = rows // NA_ROWS
    cases = []
    for blk in (0, min(1, n_blk - 1), n_blk - 1):
        r0 = blk * NA_ROWS
        ks = int(np.clip(r0 - WIN_R // 2, 0, rows - NA_KROWS))
        q_rows = []
        for qr in range(NA_ROWS):
            r = r0 + qr
            rs = int(np.clip(r - WIN_R // 2, 0, rows - WIN_R))
            tiles = []
            for a in range(NA_KROWS):
                kr = ks + a
                tiles.append(t[:, kr - r + WIN_R - 1] if rs <= kr < rs + WIN_R else masked)
            q_rows.append(jnp.concatenate(tiles, axis=-1))
        cases.append(jnp.concatenate(q_rows, axis=-2))
    return jnp.stack(cases, axis=0).astype(F32)


def _routing_plan(rt, n_blocks):
    ntok = rt.shape[0]
    eid = rt[:, 0:2].astype(jnp.int32).reshape(-1)
    onehot = (eid[:, None] == jnp.arange(N_EXPERTS)[None, :]).astype(jnp.int32)
    csum = jnp.cumsum(onehot, axis=0)
    counts = csum[-1]
    rank = jnp.take_along_axis(csum, eid[:, None], axis=1)[:, 0] - 1
    padded = ((counts + MOE_BLK - 1) // MOE_BLK) * MOE_BLK
    pad_end = jnp.cumsum(padded)
    pad_start = pad_end - padded
    dest8 = ((pad_start[eid] + rank) * ROW_SUB).astype(jnp.int32)
    blk_start = jnp.arange(n_blocks, dtype=jnp.int32) * MOE_BLK
    blk_e = jnp.clip(jnp.sum(pad_end[None, :] <= blk_start[:, None], axis=1), 0, N_EXPERTS - 1).astype(jnp.int32)
    nused = (pad_end[-1] // MOE_BLK).astype(jnp.int32).reshape(1)
    return blk_e, nused, dest8


def kernel(x, c, ctx, c_ctx, w_mod, b_mod, g_attn, g_ffn, w_in, conv_w, rpb, sinks, w_branch, w_o,
           w_rg, b_rg, w_re, b_re, w_e_gate, w_e_up, w_e_down, g_final):
    bsz, seq, d = x.shape
    ctx_len = ctx.shape[1]
    depth = w_in.shape[0]
    ltot = seq + ctx_len
    n_lat = seq // TOK_TILE
    n_tiles = ltot // TOK_TILE
    rows = seq // GRID_W
    ntok = bsz * ltot
    n_assign = ntok * 2
    n_blocks = (n_assign + N_EXPERTS * (MOE_BLK - 1) + MOE_BLK - 1) // MOE_BLK

    x_all = jnp.concatenate([x, ctx], axis=1)
    cos_t, sin_t = _rope_tables(seq, ctx_len)
    assert bsz + 1 <= 8
    c8 = jnp.pad(jnp.concatenate([c, c_ctx[None, :]], axis=0), ((0, 7 - bsz), (0, 0)))
    mod_all = _mod_vectors(c8, w_mod, b_mod)
    yb_rows = np.concatenate([h * HEAD_DIM + np.arange(HEAD_DIM) for h in _SW_HEAD_ORDER])

    for l in range(depth):
        mod8 = jnp.pad(mod_all[l, :bsz + 1].reshape(bsz + 1, 6, d), ((0, 0), (0, 2), (0, 0)))
        w_l = _prep_w_in(w_in[l])
        bias = _na_bias_tables(rpb[l], rows)
        o2 = D_CONV + D_NA
        wb = jnp.concatenate([w_branch[l, :o2], w_branch[l, o2:][yb_rows]], axis=0).astype(BF16)
        wr = jnp.zeros((d, LANES), F32).at[:, 0:N_GROUPS].set(w_rg[l]).at[:, 8:8 + N_EXPERTS].set(w_re[l])
        wr_hi = wr.astype(BF16)
        wr_lo = (wr - wr_hi.astype(F32)).astype(BF16)
        br = jnp.zeros((1, LANES), F32).at[0, 0:N_GROUPS].set(b_rg[l]).at[0, 8:8 + N_EXPERTS].set(b_re[l])
        convw8 = jnp.pad(conv_w[l], ((0, 8 - CONV_W), (0, 0)))

        z, ab, nq, nk, nv, wq, wk, wv, gates = _inproj(
            x_all, mod8, g_attn[l][None, :], w_l, cos_t, sin_t, n_lat)
        y_b = _na_attn(nq, nk, nv, bias, seq)
        y_c = _sw_attn(sinks[l], wq, wk, wv, seq)
        x_mid, h2t, rt = _merge(x_all, mod8, g_ffn[l][None, :], z, ab, y_b, y_c, gates, convw8, wb,
                                w_o[l].astype(BF16), wr_hi, wr_lo, br, n_lat)
        rt = rt.reshape(ntok, LANES)
        blk_e, nused, dest8 = _routing_plan(rt, n_blocks)
        xd = _dispatch(dest8, h2t, n_blocks)
        yd = _moe(blk_e, nused, xd, w_e_gate[l].astype(BF16), w_e_up[l].astype(BF16),
                  w_e_down[l].astype(BF16))
        final = l == depth - 1
        x_flat = _combine(dest8, x_mid.reshape(ntok, d), mod8, g_final[None, :], rt, yd, n_lat, n_tiles,
                          final)
        x_all = x_flat.reshape(bsz, ltot, d)
    return x_all[:, :seq]
```

```python
import functools

import numpy as np
import jax
import jax.numpy as jnp
from jax import lax
from jax.experimental import pallas as pl
from jax.experimental.pallas import tpu as pltpu

F32 = jnp.float32
BF16 = jnp.bfloat16

D_MODEL = 1024
GRID_W = 64
HEAD_DIM = 64
D_CONV = 256
CONV_W = 3
H_NA = 6
WIN_R = 8
WIN_C = 16
H_Q = 6
H_KV = 2
WINDOW = 128
ROPE_BASE = 10000.0
D_NA = H_NA * HEAD_DIM
D_SW = H_Q * HEAD_DIM
D_KV = H_KV * HEAD_DIM
N_GROUPS = 4
EXP_PER_GROUP = 8
N_EXPERTS = N_GROUPS * EXP_PER_GROUP
D_EXPERT = 256
MOE_BLK = 256
EPS = 1e-6
NEG_INF = -1e30

LANES = 128
ROW_SUB = D_MODEL // LANES
TOK_TILE = 256
INPROJ_SUB = 2
NA_ROWS = TOK_TILE // GRID_W
NA_KROWS = NA_ROWS + WIN_R - 1
NA_KEYS = NA_KROWS * GRID_W
SW_KEYS = TOK_TILE + 2 * WINDOW
VMEM_LIMIT = 56 * 1024 * 1024
DMA_UNROLL = 8
MOE_RING = 3

_C_CONV = 0
_C_NQ = _C_CONV + 3 * D_CONV
_C_NK = _C_NQ + D_NA
_C_NV = _C_NK + D_NA
_C_WQ = _C_NV + D_NA
_C_WQS = _C_WQ + D_SW
_C_WK = _C_WQS + D_SW
_C_WKS = _C_WK + D_KV
_C_WV = _C_WKS + D_KV
_C_GATE = _C_WV + D_KV
_C_END = _C_GATE + 3 * D_MODEL

_SW_HEAD_ORDER = (0, 3, 1, 4, 2, 5)


def _dot(a, b):
    return jnp.dot(a, b, preferred_element_type=F32)


def _dot_nt(a, b):
    return lax.dot_general(a, b, (((1,), (1,)), ((), ())), preferred_element_type=F32)


def _sigmoid(x):
    return 1.0 / (1.0 + jnp.exp(-x))


def _modln(x, g, shift, scale):
    ms = jnp.mean(x * x, axis=-1, keepdims=True)
    return (x * lax.rsqrt(ms + EPS)) * g * (1.0 + scale) + shift


def _store_row_tiles(ref, val):
    n = val.shape[0]
    for s in range(ROW_SUB):
        ref[pl.ds(s, n, stride=ROW_SUB), :] = val[:, s * LANES:(s + 1) * LANES]


def _load_row_tiles(ref, n):
    return jnp.concatenate([ref[pl.ds(s, n, stride=ROW_SUB), :] for s in range(ROW_SUB)], axis=-1)


def _mod_kernel(c_ref, w_ref, b_ref, o_ref):
    cv = c_ref[...]
    sc = (cv * _sigmoid(cv)).astype(BF16)
    o_ref[0] = _dot(sc, w_ref[0].astype(BF16)) + b_ref[0]


def _mod_vectors(c8, w_mod, b_mod):
    depth, d, cols = w_mod.shape
    tn = cols // 4
    return pl.pallas_call(
        _mod_kernel,
        grid=(depth, cols // tn),
        in_specs=[
            pl.BlockSpec((8, d), lambda l, j: (0, 0)),
            pl.BlockSpec((1, d, tn), lambda l, j: (l, 0, j)),
            pl.BlockSpec((1, 1, tn), lambda l, j: (l, 0, j)),
        ],
        out_specs=pl.BlockSpec((1, 8, tn), lambda l, j: (l, 0, j)),
        out_shape=jax.ShapeDtypeStruct((depth, 8, cols), F32),
        compiler_params=pltpu.CompilerParams(
            dimension_semantics=("arbitrary", "arbitrary"), vmem_limit_bytes=VMEM_LIMIT),
        name="mod_vectors",
    )(c8, w_mod, b_mod.reshape(depth, 1, cols))


def _inproj_kernel(*refs):
    ns = INPROJ_SUB
    x_ref, g_ref, w_ref = refs[0], refs[1], refs[2]
    mod_refs = refs[3:3 + ns]
    cos_refs = refs[3 + ns:3 + 2 * ns]
    sin_refs = refs[3 + 2 * ns:3 + 3 * ns]
    z_ref, ab_ref, nq_ref, nk_ref, nv_ref, wq_ref, wk_ref, wv_ref, gate_ref = refs[3 + 3 * ns:]

    h = jnp.concatenate(
        [_modln(x_ref[j * TOK_TILE:(j + 1) * TOK_TILE, :], g_ref[...],
                mod_refs[j][0, 0:1, :], mod_refs[j][0, 1:2, :]).astype(BF16) for j in range(ns)], axis=0)

    def proj(c0, c1):
        return _dot(h, w_ref[:, c0:c1])

    a_h = proj(_C_CONV, _C_CONV + D_CONV)
    a_b = proj(_C_CONV + D_CONV, _C_CONV + 2 * D_CONV)
    a_c = proj(_C_CONV + 2 * D_CONV, _C_NQ)
    z_ref[...] = (a_c * a_h).astype(BF16)
    ab_ref[...] = a_b.astype(BF16)

    qk_scale = HEAD_DIM ** -0.5
    nq_ref[...] = (proj(_C_NQ, _C_NK) * qk_scale).astype(BF16)
    nk_ref[...] = proj(_C_NK, _C_NV).astype(BF16)
    nv_ref[...] = proj(_C_NV, _C_WQ).astype(BF16)

    cos = jnp.concatenate([r[...] for r in cos_refs], axis=0)
    sin = jnp.concatenate([r[...] for r in sin_refs], axis=0)
    cos3 = jnp.concatenate([cos] * (D_SW // LANES), axis=-1)
    sin3 = jnp.concatenate([sin] * (D_SW // LANES), axis=-1)
    wq = proj(_C_WQ, _C_WQS) * cos3 + proj(_C_WQS, _C_WK) * sin3
    wq_ref[...] = (wq * qk_scale).astype(BF16)
    wk_ref[...] = (proj(_C_WK, _C_WKS) * cos + proj(_C_WKS, _C_WV) * sin).astype(BF16)
    wv_ref[...] = proj(_C_WV, _C_GATE).astype(BF16)

    for j in range(3):
        c0 = _C_GATE + j * D_MODEL
        gate_ref[:, j * D_MODEL:(j + 1) * D_MODEL] = _sigmoid(proj(c0, c0 + D_MODEL)).astype(BF16)


def _inproj(x_all, mod8, g, w, cos_t, sin_t, n_lat):
    bsz, ltot, d = x_all.shape
    n_tiles = ltot // TOK_TILE
    ns = INPROJ_SUB
    tm = ns * TOK_TILE
    ntok = bsz * ltot
    assert ntok % tm == 0

    def tok_spec(width):
        return pl.BlockSpec((tm, width), lambda t: (t, 0))

    def mod_spec(j):
        def index(t):
            u = t * ns + j
            return (jnp.where(u % n_tiles < n_lat, u // n_tiles, bsz), 0, 0)
        return pl.BlockSpec((1, 8, d), index)

    def rope_spec(j):
        return pl.BlockSpec((TOK_TILE, LANES), lambda t: ((t * ns + j) % n_tiles, 0))

    widths = (D_CONV, D_CONV, D_NA, D_NA, D_NA, D_SW, D_KV, D_KV, 3 * D_MODEL)
    outs = pl.pallas_call(
        _inproj_kernel,
        grid=(ntok // tm,),
        in_specs=([tok_spec(d),
                   pl.BlockSpec((1, d), lambda t: (0, 0)),
                   pl.BlockSpec((d, _C_END), lambda t: (0, 0), pipeline_mode=pl.Buffered(1))]
                  + [mod_spec(j) for j in range(ns)]
                  + [rope_spec(j) for j in range(ns)]
                  + [rope_spec(j) for j in range(ns)]),
        out_specs=[tok_spec(wd) for wd in widths],
        out_shape=[jax.ShapeDtypeStruct((ntok, wd), BF16) for wd in widths],
        compiler_params=pltpu.CompilerParams(
            dimension_semantics=("arbitrary",), vmem_limit_bytes=VMEM_LIMIT),
        name="inproj",
    )(x_all.reshape(ntok, d), g, w, *([mod8] * ns), *([cos_t] * ns), *([sin_t] * ns))
    return [o.reshape(bsz, ltot, o.shape[-1]) for o in outs]


def _half_mask(half):
    lane = lax.broadcasted_iota(jnp.int32, (1, LANES), 1)
    return (lane < HEAD_DIM) if half == 0 else (lane >= HEAD_DIM)


def _softmax_pv(parts, sink=None):
    m = parts[0][0].max(axis=-1, keepdims=True)
    for s, _ in parts[1:]:
        m = jnp.maximum(m, s.max(axis=-1, keepdims=True))
    if sink is not None:
        m = jnp.maximum(m, sink)
    l = None
    o = None
    for s, v in parts:
        e = jnp.exp(s - m)
        ls = e.sum(axis=-1, keepdims=True)
        os_ = _dot(e.astype(BF16), v)
        l = ls if l is None else l + ls
        o = os_ if o is None else o + os_
    if sink is not None:
        l = l + jnp.exp(sink - m)
    return o * (1.0 / l)


def _na_kernel(q_ref, k_ref, v_ref, bias_ref, o_ref, *, n_blk, rows, seq):
    i = pl.program_id(1)
    ctx_len = k_ref.shape[1] - seq

    def pair_slices(p):
        return slice(p * LANES, (p + 1) * LANES)

    @pl.when(i < n_blk)
    def _latent():
        r0 = i * NA_ROWS
        ks = jnp.clip(r0 - WIN_R // 2, 0, rows - NA_KROWS)
        kstart = pl.multiple_of(ks * GRID_W, GRID_W)
        for p in range(H_NA // 2):
            ps = pair_slices(p)
            q2 = q_ref[0, :, ps]
            k2 = k_ref[0, pl.ds(kstart, NA_KEYS), ps]
            v2 = v_ref[0, pl.ds(kstart, NA_KEYS), ps]
            kc2 = k_ref[0, seq:seq + ctx_len, ps]
            vc2 = v_ref[0, seq:seq + ctx_len, ps]
            outs = []
            for half in range(2):
                qm = jnp.where(_half_mask(half), q2, jnp.zeros_like(q2))
                s = _dot_nt(qm, k2) + bias_ref[0, 2 * p + half]
                sc = _dot_nt(qm, kc2)
                outs.append(_softmax_pv([(s, v2), (sc, vc2)]))
            o_ref[0, :, ps] = jnp.where(_half_mask(0), outs[0], outs[1]).astype(BF16)

    @pl.when(i == n_blk)
    def _context():
        for p in range(H_NA // 2):
            ps = pair_slices(p)
            q2 = q_ref[0, :, ps]
            kc2 = k_ref[0, seq:seq + ctx_len, ps]
            vc2 = v_ref[0, seq:seq + ctx_len, ps]
            outs = []
            for half in range(2):
                qm = jnp.where(_half_mask(half), q2, jnp.zeros_like(q2))
                outs.append(_softmax_pv([(_dot_nt(qm, kc2), vc2)]))
            o_ref[0, :, ps] = jnp.where(_half_mask(0), outs[0], outs[1]).astype(BF16)


def _na_attn(nq, nk, nv, bias, seq):
    bsz, ltot, _ = nq.shape
    rows = seq // GRID_W
    n_blk = seq // TOK_TILE
    n_steps = ltot // TOK_TILE

    def bias_map(b, i):
        return (jnp.where(i == 0, 0, jnp.where(i >= n_blk - 1, 2, 1)), 0, 0, 0)

    return pl.pallas_call(
        functools.partial(_na_kernel, n_blk=n_blk, rows=rows, seq=seq),
        grid=(bsz, n_steps),
        in_specs=[
            pl.BlockSpec((1, TOK_TILE, D_NA), lambda b, i: (b, i, 0)),
            pl.BlockSpec((1, ltot, D_NA), lambda b, i: (b, 0, 0)),
            pl.BlockSpec((1, ltot, D_NA), lambda b, i: (b, 0, 0)),
            pl.BlockSpec((1, H_NA, TOK_TILE, NA_KEYS), bias_map),
        ],
        out_specs=pl.BlockSpec((1, TOK_TILE, D_NA), lambda b, i: (b, i, 0)),
        out_shape=jax.ShapeDtypeStruct((bsz, ltot, D_NA), BF16),
        compiler_params=pltpu.CompilerParams(
            dimension_semantics=("arbitrary", "arbitrary"), vmem_limit_bytes=VMEM_LIMIT),
        name="na_attn",
    )(nq, nk, nv, bias)


def _sw_kernel(sink_ref, q_ref, k_ref, v_ref, o_ref, *, n_blk, seq):
    i = pl.program_id(1)
    ctx_len = k_ref.shape[1] - seq
    kc2 = k_ref[0, seq:seq + ctx_len, :]
    vc2 = v_ref[0, seq:seq + ctx_len, :]

    def heads_of_pair(p, window_parts):
        q2 = q_ref[0, :, p * LANES:(p + 1) * LANES]
        outs = []
        for half in range(2):
            sink = sink_ref[half * (H_Q // H_KV) + p]
            qm = jnp.where(_half_mask(half), q2, jnp.zeros_like(q2))
            parts = [(fn(qm), v) for fn, v in window_parts] + [(_dot_nt(qm, kc2), vc2)]
            outs.append(_softmax_pv(parts, sink=sink))
        o_ref[0, :, p * LANES:(p + 1) * LANES] = jnp.where(_half_mask(0), outs[0], outs[1]).astype(BF16)

    @pl.when(i < n_blk)
    def _latent():
        t0 = i * TOK_TILE
        kst = pl.multiple_of(jnp.clip(t0 - WINDOW, 0, seq - SW_KEYS), WINDOW)
        qpos = t0 + lax.broadcasted_iota(jnp.int32, (TOK_TILE, 1), 0)
        kpos = kst + lax.broadcasted_iota(jnp.int32, (1, SW_KEYS), 1)
        valid = jnp.abs(qpos - kpos) <= WINDOW
        k2 = k_ref[0, pl.ds(kst, SW_KEYS), :]
        v2 = v_ref[0, pl.ds(kst, SW_KEYS), :]
        for p in range(H_Q // 2):
            heads_of_pair(p, [(lambda qm: jnp.where(valid, _dot_nt(qm, k2), NEG_INF), v2)])

    @pl.when(i == n_blk)
    def _context():
        for p in range(H_Q // 2):
            heads_of_pair(p, [])


def _sw_attn(sinks, wq, wk, wv, seq):
    bsz, ltot, _ = wq.shape
    n_blk = seq // TOK_TILE
    n_steps = ltot // TOK_TILE
    return pl.pallas_call(
        functools.partial(_sw_kernel, n_blk=n_blk, seq=seq),
        grid_spec=pltpu.PrefetchScalarGridSpec(
            num_scalar_prefetch=1,
            grid=(bsz, n_steps),
            in_specs=[
                pl.BlockSpec((1, TOK_TILE, D_SW), lambda b, i, s: (b, i, 0)),
                pl.BlockSpec((1, ltot, D_KV), lambda b, i, s: (b, 0, 0)),
                pl.BlockSpec((1, ltot, D_KV), lambda b, i, s: (b, 0, 0)),
            ],
            out_specs=pl.BlockSpec((1, TOK_TILE, D_SW), lambda b, i, s: (b, i, 0)),
        ),
        out_shape=jax.ShapeDtypeStruct((bsz, ltot, D_SW), BF16),
        compiler_params=pltpu.CompilerParams(
            dimension_semantics=("arbitrary", "arbitrary"), vmem_limit_bytes=VMEM_LIMIT),
        name="sw_attn",
    )(sinks, wq, wk, wv)


def _split_bf16(x):
    hi = x.astype(BF16)
    lo = (x - hi.astype(F32)).astype(BF16)
    return hi, lo


def _merge_kernel(x_ref, mod_ref, g_ref, z_ref, zp_ref, zn_ref, ab_ref, yb_ref, yc_ref, gate_ref,
                  convw_ref, wb_ref, wo_ref, wr_hi_ref, wr_lo_ref, br_ref,
                  xo_ref, h2_ref, rt_ref, *, n_lat):
    i = pl.program_id(1)
    n_tiles = pl.num_programs(1)
    tm = x_ref.shape[1]

    z = z_ref[0].astype(F32)
    halo = zp_ref.shape[1]
    first = (i == 0) | (i == n_lat)
    last = (i == n_lat - 1) | (i == n_tiles - 1)
    prev_row = jnp.where(first, 0.0, zp_ref[0, halo - 1:halo, :].astype(F32))
    next_row = jnp.where(last, 0.0, zn_ref[0, 0:1, :].astype(F32))
    row = lax.broadcasted_iota(jnp.int32, (tm, 1), 0)
    z_prev = jnp.where(row == 0, prev_row, pltpu.roll(z, 1, axis=0))
    z_next = jnp.where(row == tm - 1, next_row, pltpu.roll(z, tm - 1, axis=0))
    cw = convw_ref[...]
    y_a = ab_ref[0].astype(F32) * (z_prev * cw[0:1] + z * cw[1:2] + z_next * cw[2:3])

    o1 = D_CONV
    o2 = D_CONV + D_NA
    g_a = gate_ref[0, :, 0:D_MODEL].astype(F32)
    g_b = gate_ref[0, :, D_MODEL:2 * D_MODEL].astype(F32)
    g_c = gate_ref[0, :, 2 * D_MODEL:3 * D_MODEL].astype(F32)
    m = (g_a * _dot(y_a.astype(BF16), wb_ref[0:o1, :])
         + g_b * _dot(yb_ref[0], wb_ref[o1:o2, :])
         + g_c * _dot(yc_ref[0], wb_ref[o2:, :]))
    y = _dot(m.astype(BF16), wo_ref[...])
    x_new = x_ref[0] + mod_ref[0, 2:3, :] * y
    xo_ref[0] = x_new

    h2 = _modln(x_new, g_ref[...], mod_ref[0, 3:4, :], mod_ref[0, 4:5, :])
    _store_row_tiles(h2_ref, h2)

    h_hi, h_lo = _split_bf16(h2)
    lg = (_dot(h_hi, wr_hi_ref[...]) + _dot(h_hi, wr_lo_ref[...]) + _dot(h_lo, wr_hi_ref[...])
          + br_ref[...])
    lane = lax.broadcasted_iota(jnp.int32, lg.shape, 1)
    big = jnp.int32(LANES)

    def masked_argmax(vals, mask):
        v = jnp.where(mask, vals, -jnp.inf)
        vmax = v.max(axis=-1, keepdims=True)
        idx = jnp.where(mask & (v == vmax), lane, big).min(axis=-1, keepdims=True)
        return vmax, idx

    is_grp = lane < N_GROUPS
    g_max, g_idx = masked_argmax(lg, is_grp)
    pg_top = 1.0 / jnp.where(is_grp, jnp.exp(lg - g_max), 0.0).sum(axis=-1, keepdims=True)
    e_lo = 8 + g_idx * EXP_PER_GROUP
    in_grp = (lane >= e_lo) & (lane < e_lo + EXP_PER_GROUP)
    v1, i1 = masked_argmax(lg, in_grp)
    v2, i2 = masked_argmax(lg, in_grp & (lane != i1))
    e21 = jnp.exp(v2 - v1)
    w1 = pg_top / (1.0 + e21)
    w2 = pg_top * e21 / (1.0 + e21)
    rt = jnp.where(lane == 0, (i1 - 8).astype(F32),
                   jnp.where(lane == 1, (i2 - 8).astype(F32),
                             jnp.where(lane == 2, w1, jnp.where(lane == 3, w2, 0.0))))
    rt_ref[0] = rt


def _merge(x_all, mod8, g, z, ab, y_b, y_c, gates, conv_w, wb, wo, wr_hi, wr_lo, br, n_lat):
    bsz, ltot, d = x_all.shape
    n_tiles = ltot // TOK_TILE
    tm = TOK_TILE
    halo = 16
    hb = tm // halo
    n_halo = ltot // halo

    def tok_spec(width):
        return pl.BlockSpec((1, tm, width), lambda b, i: (b, i, 0))

    def mod_map(b, i):
        return (jnp.where(i < n_lat, b, bsz), 0, 0)

    def const2(shape):
        return pl.BlockSpec(shape, lambda b, i: (0, 0))

    return pl.pallas_call(
        functools.partial(_merge_kernel, n_lat=n_lat),
        grid=(bsz, n_tiles),
        in_specs=[
            tok_spec(d),
            pl.BlockSpec((1, 8, d), mod_map),
            const2((1, d)),
            tok_spec(D_CONV),
            pl.BlockSpec((1, halo, D_CONV), lambda b, i: (b, jnp.maximum(i * hb - 1, 0), 0)),
            pl.BlockSpec((1, halo, D_CONV), lambda b, i: (b, jnp.minimum((i + 1) * hb, n_halo - 1), 0)),
            tok_spec(D_CONV),
            tok_spec(D_NA),
            tok_spec(D_SW),
            tok_spec(3 * d),
            const2((8, D_CONV)),
            const2((d, d)),
            const2((d, d)),
            const2((d, LANES)),
            const2((d, LANES)),
            const2((1, LANES)),
        ],
        out_specs=[tok_spec(d),
                   pl.BlockSpec((tm * ROW_SUB, LANES), lambda b, i: (b * n_tiles + i, 0)),
                   tok_spec(LANES)],
        out_shape=[jax.ShapeDtypeStruct((bsz, ltot, d), F32),
                   jax.ShapeDtypeStruct((bsz * ltot * ROW_SUB, LANES), F32),
                   jax.ShapeDtypeStruct((bsz, ltot, LANES), F32)],
        compiler_params=pltpu.CompilerParams(
            dimension_semantics=("arbitrary", "arbitrary"), vmem_limit_bytes=VMEM_LIMIT),
        name="merge",
    )(x_all, mod8, g, z, z, z, ab, y_b, y_c, gates, conv_w, wb, wo, wr_hi, wr_lo, br)


def _dispatch_kernel(dest_ref, pad_lo_ref, pad_hi_ref, h2_ref, xd_hbm, zero_buf, sem, zsem,
                     *, pads_per_step):
    s = pl.program_id(0)
    tm = h2_ref.shape[0] // ROW_SUB

    def row_copy(r, k):
        dst = pl.multiple_of(dest_ref[(s * tm + r) * 2 + k], ROW_SUB)
        src = pl.multiple_of(r * ROW_SUB, ROW_SUB)
        return pltpu.make_async_copy(h2_ref.at[pl.ds(src, ROW_SUB)], xd_hbm.at[pl.ds(dst, ROW_SUB)], sem)

    def body(r, carry):
        for k in range(2):
            row_copy(r, k).start(priority=k)
        return carry
    lax.fori_loop(0, tm, body, 0, unroll=DMA_UNROLL)

    def zero_copy(row):
        dst = pl.multiple_of(row * ROW_SUB, ROW_SUB)
        return pltpu.make_async_copy(zero_buf, xd_hbm.at[pl.ds(dst, ROW_SUB)], zsem)

    n_ranges = pad_lo_ref.shape[0]
    for j in range(pads_per_step):
        idx = s * pads_per_step + j

        @pl.when(idx < n_ranges)
        def _fill():
            zero_buf[...] = jnp.zeros_like(zero_buf)
            lo = pad_lo_ref[idx]
            hi = pad_hi_ref[idx]
            lax.fori_loop(lo, hi, lambda row, c: (zero_copy(row).start(), c)[1], 0)
            lax.fori_loop(lo, hi, lambda row, c: (zero_copy(row).wait(), c)[1], 0)

    for k in range(2):
        pltpu.make_async_copy(h2_ref, xd_hbm.at[pl.ds(0, tm * ROW_SUB)], sem).wait()


def _dispatch(dest8, pad_lo, pad_hi, h2t, n_blocks):
    tm = TOK_TILE
    n_steps = h2t.shape[0] // (tm * ROW_SUB)
    xd_rows = n_blocks * MOE_BLK * ROW_SUB
    pads_per_step = -(-pad_lo.shape[0] // n_steps)
    return pl.pallas_call(
        functools.partial(_dispatch_kernel, pads_per_step=pads_per_step),
        grid_spec=pltpu.PrefetchScalarGridSpec(
            num_scalar_prefetch=3,
            grid=(n_steps,),
            in_specs=[pl.BlockSpec((tm * ROW_SUB, LANES), lambda s, d8, plo, phi: (s, 0))],
            out_specs=pl.BlockSpec(memory_space=pl.ANY),
            scratch_shapes=[pltpu.VMEM((ROW_SUB, LANES), F32), pltpu.SemaphoreType.DMA(()),
                            pltpu.SemaphoreType.DMA(())],
        ),
        out_shape=jax.ShapeDtypeStruct((xd_rows, LANES), F32),
        compiler_params=pltpu.CompilerParams(
            dimension_semantics=("arbitrary",), vmem_limit_bytes=VMEM_LIMIT),
        name="moe_dispatch",
    )(dest8, pad_lo, pad_hi, h2t)


def _moe_kernel(blk_e_ref, nused_ref, xd_hbm, wg_ref, wu_ref, wd_ref, yd_ref, wg_s, wu_s, wd_s,
                xbuf, xsem):
    i = pl.program_id(0)
    nused = nused_ref[0]
    blk_rows = MOE_BLK * ROW_SUB

    def block_copy(j):
        src = pl.multiple_of(j * blk_rows, blk_rows)
        return pltpu.make_async_copy(xd_hbm.at[pl.ds(src, blk_rows)], xbuf.at[j % MOE_RING],
                                     xsem.at[j % MOE_RING])

    for j in range(MOE_RING - 1):
        @pl.when((i == 0) & (j < nused))
        def _prime():
            block_copy(j).start()

    @pl.when(i + MOE_RING - 1 < nused)
    def _prefetch():
        block_copy(i + MOE_RING - 1).start()

    @pl.when((i == 0) | (blk_e_ref[i] != blk_e_ref[jnp.maximum(i - 1, 0)]))
    def _cast_weights():
        wg_s[...] = wg_ref[0].astype(BF16)
        wu_s[...] = wu_ref[0].astype(BF16)
        wd_s[...] = wd_ref[0].astype(BF16)

    @pl.when(i < nused)
    def _compute():
        block_copy(i).wait()
        xb = _load_row_tiles(xbuf.at[i % MOE_RING], MOE_BLK).astype(BF16)
        gate = _dot(xb, wg_s[...])
        up = _dot(xb, wu_s[...])
        hid = (gate * _sigmoid(gate)) * up
        _store_row_tiles(yd_ref, _dot(hid.astype(BF16), wd_s[...]))

    @pl.when(i >= nused)
    def _unused():
        yd_ref[...] = jnp.zeros_like(yd_ref)


def _moe(blk_e, nused, xd, wg, wu, wd):
    nb = blk_e.shape[0]
    d = wg.shape[1]
    blk_rows = MOE_BLK * ROW_SUB

    return pl.pallas_call(
        _moe_kernel,
        grid_spec=pltpu.PrefetchScalarGridSpec(
            num_scalar_prefetch=2,
            grid=(nb,),
            in_specs=[
                pl.BlockSpec(memory_space=pl.ANY),
                pl.BlockSpec((1, d, D_EXPERT), lambda i, be, nu: (be[i], 0, 0)),
                pl.BlockSpec((1, d, D_EXPERT), lambda i, be, nu: (be[i], 0, 0)),
                pl.BlockSpec((1, D_EXPERT, d), lambda i, be, nu: (be[i], 0, 0)),
            ],
            out_specs=pl.BlockSpec((blk_rows, LANES), lambda i, be, nu: (i, 0)),
            scratch_shapes=[pltpu.VMEM((d, D_EXPERT), BF16), pltpu.VMEM((d, D_EXPERT), BF16),
                            pltpu.VMEM((D_EXPERT, d), BF16),
                            pltpu.VMEM((MOE_RING, blk_rows, LANES), F32),
                            pltpu.SemaphoreType.DMA((MOE_RING,))],
        ),
        out_shape=jax.ShapeDtypeStruct((nb * blk_rows, LANES), F32),
        compiler_params=pltpu.CompilerParams(
            dimension_semantics=("arbitrary",), vmem_limit_bytes=VMEM_LIMIT),
        name="moe_experts",
    )(blk_e, nused, xd, wg, wu, wd)


def _combine_kernel(dest_ref, x_ref, mod_ref, g_ref, rt_ref, yd_hbm, o_ref, buf, sem, *, final, tile_of):
    s = pl.program_id(0)
    n_steps = pl.num_programs(0)
    tm = x_ref.shape[0]
    slot = s % 2

    def start_gather(step, dst_slot):
        base = tile_of(step) * tm

        def body(r, carry):
            for k in range(2):
                src = pl.multiple_of(dest_ref[(base + r) * 2 + k], ROW_SUB)
                dst = pl.multiple_of(r * ROW_SUB, ROW_SUB)
                pltpu.make_async_copy(yd_hbm.at[pl.ds(src, ROW_SUB)], buf.at[dst_slot, k, pl.ds(dst, ROW_SUB)],
                                      sem.at[dst_slot]).start(priority=k)
            return carry
        lax.fori_loop(0, tm, body, 0, unroll=DMA_UNROLL)

    @pl.when(s == 0)
    def _prime():
        start_gather(0, 0)

    @pl.when(s + 1 < n_steps)
    def _prefetch():
        start_gather(s + 1, 1 - slot)

    for k in range(2):
        pltpu.make_async_copy(yd_hbm.at[pl.ds(0, tm * ROW_SUB)], buf.at[slot, k], sem.at[slot]).wait()
    w0 = rt_ref[:, 2:3]
    w1 = rt_ref[:, 3:4]
    y = w0 * _load_row_tiles(buf.at[slot, 0], tm) + w1 * _load_row_tiles(buf.at[slot, 1], tm)
    x_new = x_ref[...] + mod_ref[0, 5:6, :] * y
    if final:
        ms = jnp.mean(x_new * x_new, axis=-1, keepdims=True)
        x_new = x_new * lax.rsqrt(ms + EPS) * g_ref[...]
    o_ref[...] = x_new


def _combine(dest, x_flat, mod8, g_final, rt, yd, n_lat, n_tiles, final):
    ntok, d = x_flat.shape
    tm = TOK_TILE
    bsz = ntok // (tm * n_tiles)
    n_per = n_lat if final else n_tiles
    n_steps = bsz * n_per

    def tile_of(step):
        return (step // n_per) * n_tiles + step % n_per

    def mod_map(s, dref):
        return (jnp.where(s % n_per < n_lat, s // n_per, bsz), 0, 0)

    return pl.pallas_call(
        functools.partial(_combine_kernel, final=final, tile_of=tile_of),
        grid_spec=pltpu.PrefetchScalarGridSpec(
            num_scalar_prefetch=1,
            grid=(n_steps,),
            in_specs=[
                pl.BlockSpec((tm, d), lambda s, dref: (tile_of(s), 0)),
                pl.BlockSpec((1, 8, d), mod_map),
                pl.BlockSpec((1, d), lambda s, dref: (0, 0)),
                pl.BlockSpec((tm, LANES), lambda s, dref: (tile_of(s), 0)),
                pl.BlockSpec(memory_space=pl.ANY),
            ],
            out_specs=pl.BlockSpec((tm, d), lambda s, dref: (s, 0)),
            scratch_shapes=[pltpu.VMEM((2, 2, tm * ROW_SUB, LANES), F32), pltpu.SemaphoreType.DMA((2,))],
        ),
        out_shape=jax.ShapeDtypeStruct((n_steps * tm, d), F32),
        compiler_params=pltpu.CompilerParams(
            dimension_semantics=("arbitrary",), vmem_limit_bytes=VMEM_LIMIT),
        name="combine",
    )(dest, x_flat, mod8, g_final, rt, yd)


def _rope_swap(w, n_heads):
    d = w.shape[0]
    quarter = HEAD_DIM // 4
    return jnp.flip(w.reshape(d, n_heads, 2, 2, quarter), axis=3).reshape(d, n_heads * HEAD_DIM)


def _prep_w_in(w_in):
    sizes = (3 * D_CONV + 3 * D_NA, D_SW, D_KV, D_KV, 3 * D_MODEL)
    offs = np.concatenate([[0], np.cumsum(sizes)])
    front, wq, wk, wv, gates = [w_in[:, offs[j]:offs[j + 1]].astype(BF16) for j in range(len(sizes))]
    wq = jnp.concatenate([wq[:, h * HEAD_DIM:(h + 1) * HEAD_DIM] for h in _SW_HEAD_ORDER], axis=1)
    return jnp.concatenate([front, wq, _rope_swap(wq, H_Q), wk, _rope_swap(wk, H_KV), wv, gates], axis=1)


def _rope_tables(seq, ctx_len):
    half = HEAD_DIM // 2
    quarter = half // 2
    t = jnp.arange(seq)
    row = (t // GRID_W).astype(F32)
    col = (t % GRID_W).astype(F32)
    inv = ROPE_BASE ** (-jnp.arange(quarter, dtype=F32) / quarter)
    ang_r = row[:, None] * inv[None, :]
    ang_c = col[:, None] * inv[None, :]
    cos = jnp.concatenate([jnp.cos(ang_r)] * 2 + [jnp.cos(ang_c)] * 2, axis=-1)
    sin = jnp.concatenate([-jnp.sin(ang_r), jnp.sin(ang_r), -jnp.sin(ang_c), jnp.sin(ang_c)], axis=-1)
    cos = jnp.concatenate([cos, jnp.ones((ctx_len, HEAD_DIM), F32)], axis=0)
    sin = jnp.concatenate([sin, jnp.zeros((ctx_len, HEAD_DIM), F32)], axis=0)
    return jnp.concatenate([cos, cos], axis=-1), jnp.concatenate([sin, sin], axis=-1)


def _na_bias_tables(rpb, rows):
    qc = np.arange(GRID_W)
    kc = np.arange(GRID_W)
    cstart = np.clip(qc - WIN_C // 2, 0, GRID_W - WIN_C)
    rel_c = np.clip(kc[None, :] - qc[:, None] + WIN_C - 1, 0, 2 * WIN_C - 2)
    col_ok = (kc[None, :] >= cstart[:, None]) & (kc[None, :] < cstart[:, None] + WIN_C)
    t = jnp.where(col_ok[None, None], rpb[:, :, rel_c], NEG_INF)
    masked = jnp.full((rpb.shape[0], GRID_W, GRID_W), NEG_INF, F32)
    n_blk = rows // NA_ROWS
    cases = []
    for blk in (0, min(1, n_blk - 1), n_blk - 1):
        r0 = blk * NA_ROWS
        ks = int(np.clip(r0 - WIN_R // 2, 0, rows - NA_KROWS))
        q_rows = []
        for qr in range(NA_ROWS):
            r = r0 + qr
            rs = int(np.clip(r - WIN_R // 2, 0, rows - WIN_R))
            tiles = []
            for a in range(NA_KROWS):
                kr = ks + a
                tiles.append(t[:, kr - r + WIN_R - 1] if rs <= kr < rs + WIN_R else masked)
            q_rows.append(jnp.concatenate(tiles, axis=-1))
        cases.append(jnp.concatenate(q_rows, axis=-2))
    return jnp.stack(cases, axis=0).astype(F32)


def _routing_plan(rt, n_blocks):
    ntok = rt.shape[0]
    eid = rt[:, 0:2].astype(jnp.int32).reshape(-1)
    onehot = (eid[:, None] == jnp.arange(N_EXPERTS)[None, :]).astype(jnp.int32)
    csum = jnp.cumsum(onehot, axis=0)
    counts = csum[-1]
    rank = jnp.take_along_axis(csum, eid[:, None], axis=1)[:, 0] - 1
    padded = ((counts + MOE_BLK - 1) // MOE_BLK) * MOE_BLK
    pad_end = jnp.cumsum(padded)
    pad_start = pad_end - padded
    dest8 = ((pad_start[eid] + rank) * ROW_SUB).astype(jnp.int32)
    blk_start = jnp.arange(n_blocks, dtype=jnp.int32) * MOE_BLK
    blk_e = jnp.clip(jnp.sum(pad_end[None, :] <= blk_start[:, None], axis=1), 0, N_EXPERTS - 1).astype(jnp.int32)
    nused = (pad_end[-1] // MOE_BLK).astype(jnp.int32).reshape(1)
    total = jnp.full((1,), n_blocks * MOE_BLK, jnp.int32)
    pad_lo = jnp.concatenate([pad_start + counts, pad_end[-1:]]).astype(jnp.int32)
    pad_hi = jnp.concatenate([pad_end, total]).astype(jnp.int32)
    return blk_e, nused, dest8, pad_lo, pad_hi


def kernel(x, c, ctx, c_ctx, w_mod, b_mod, g_attn, g_ffn, w_in, conv_w, rpb, sinks, w_branch, w_o,
           w_rg, b_rg, w_re, b_re, w_e_gate, w_e_up, w_e_down, g_final):
    bsz, seq, d = x.shape
    ctx_len = ctx.shape[1]
    depth = w_in.shape[0]
    ltot = seq + ctx_len
    n_lat = seq // TOK_TILE
    n_tiles = ltot // TOK_TILE
    rows = seq // GRID_W
    ntok = bsz * ltot
    n_assign = ntok * 2
    n_blocks = (n_assign + N_EXPERTS * (MOE_BLK - 1) + MOE_BLK - 1) // MOE_BLK

    x_all = jnp.concatenate([x, ctx], axis=1)
    cos_t, sin_t = _rope_tables(seq, ctx_len)
    assert bsz + 1 <= 8
    c8 = jnp.pad(jnp.concatenate([c, c_ctx[None, :]], axis=0), ((0, 7 - bsz), (0, 0)))
    mod_all = _mod_vectors(c8, w_mod, b_mod)
    yb_rows = np.concatenate([h * HEAD_DIM + np.arange(HEAD_DIM) for h in _SW_HEAD_ORDER])

    for l in range(depth):
        mod8 = jnp.pad(mod_all[l, :bsz + 1].reshape(bsz + 1, 6, d), ((0, 0), (0, 2), (0, 0)))
        w_l = _prep_w_in(w_in[l])
        bias = _na_bias_tables(rpb[l], rows)
        o2 = D_CONV + D_NA
        wb = jnp.concatenate([w_branch[l, :o2], w_branch[l, o2:][yb_rows]], axis=0).astype(BF16)
        wr = jnp.zeros((d, LANES), F32).at[:, 0:N_GROUPS].set(w_rg[l]).at[:, 8:8 + N_EXPERTS].set(w_re[l])
        wr_hi = wr.astype(BF16)
        wr_lo = (wr - wr_hi.astype(F32)).astype(BF16)
        br = jnp.zeros((1, LANES), F32).at[0, 0:N_GROUPS].set(b_rg[l]).at[0, 8:8 + N_EXPERTS].set(b_re[l])
        convw8 = jnp.pad(conv_w[l], ((0, 8 - CONV_W), (0, 0)))

        z, ab, nq, nk, nv, wq, wk, wv, gates = _inproj(
            x_all, mod8, g_attn[l][None, :], w_l, cos_t, sin_t, n_lat)
        y_b = _na_attn(nq, nk, nv, bias, seq)
        y_c = _sw_attn(sinks[l], wq, wk, wv, seq)
        x_mid, h2t, rt = _merge(x_all, mod8, g_ffn[l][None, :], z, ab, y_b, y_c, gates, convw8, wb,
                                w_o[l].astype(BF16), wr_hi, wr_lo, br, n_lat)
        rt = rt.reshape(ntok, LANES)
        blk_e, nused, dest8, pad_lo, pad_hi = _routing_plan(rt, n_blocks)
        xd = _dispatch(dest8, pad_lo, pad_hi, h2t, n_blocks)
        yd = _moe(blk_e, nused, xd, w_e_gate[l], w_e_up[l], w_e_down[l])
        final = l == depth - 1
        x_flat = _combine(dest8, x_mid.reshape(ntok, d), mod8, g_final[None, :], rt, yd, n_lat, n_tiles,
                          final)
        if not final:
            x_all = x_flat.reshape(bsz, ltot, d)
    return x_flat.reshape(bsz, seq, d)
```

```python
import functools

import numpy as np
import jax
import jax.numpy as jnp
from jax import lax
from jax.experimental import pallas as pl
from jax.experimental.pallas import tpu as pltpu

F32 = jnp.float32
BF16 = jnp.bfloat16

D_MODEL = 1024
GRID_W = 64
HEAD_DIM = 64
D_CONV = 256
CONV_W = 3
H_NA = 6
WIN_R = 8
WIN_C = 16
H_Q = 6
H_KV = 2
WINDOW = 128
ROPE_BASE = 10000.0
D_NA = H_NA * HEAD_DIM
D_SW = H_Q * HEAD_DIM
D_KV = H_KV * HEAD_DIM
N_GROUPS = 4
EXP_PER_GROUP = 8
N_EXPERTS = N_GROUPS * EXP_PER_GROUP
D_EXPERT = 256
MOE_BLK = 256
EPS = 1e-6
NEG_INF = -1e30

LANES = 128
ROW_SUB = D_MODEL // LANES
TOK_TILE = 256
INPROJ_SUB = 2
NA_ROWS = TOK_TILE // GRID_W
NA_KROWS = NA_ROWS + WIN_R - 1
NA_KEYS = NA_KROWS * GRID_W
SW_KEYS = TOK_TILE + 2 * WINDOW
VMEM_LIMIT = 56 * 1024 * 1024
DMA_UNROLL = 8

_C_CONV = 0
_C_NQ = _C_CONV + 3 * D_CONV
_C_NK = _C_NQ + D_NA
_C_NV = _C_NK + D_NA
_C_WQ = _C_NV + D_NA
_C_WK = _C_WQ + D_SW
_C_WV = _C_WK + D_KV
_C_GATE = _C_WV + D_KV
_C_END = _C_GATE + 3 * D_MODEL


def _dot(a, b):
    return jnp.dot(a, b, preferred_element_type=F32)


def _dot_nt(a, b):
    return lax.dot_general(a, b, (((1,), (1,)), ((), ())), preferred_element_type=F32)


def _sigmoid(x):
    return 1.0 / (1.0 + jnp.exp(-x))


def _modln(x, g, shift, scale):
    ms = jnp.mean(x * x, axis=-1, keepdims=True)
    return (x * lax.rsqrt(ms + EPS)) * g * (1.0 + scale) + shift


def _store_row_tiles(ref, val):
    n = val.shape[0]
    for s in range(ROW_SUB):
        ref[pl.ds(s, n, stride=ROW_SUB), :] = val[:, s * LANES:(s + 1) * LANES]


def _load_row_tiles(ref, n):
    return jnp.concatenate([ref[pl.ds(s, n, stride=ROW_SUB), :] for s in range(ROW_SUB)], axis=-1)


def _mod_kernel(c_ref, w_ref, b_ref, o_ref):
    cv = c_ref[...]
    sc = (cv * _sigmoid(cv)).astype(BF16)
    o_ref[0] = _dot(sc, w_ref[0].astype(BF16)) + b_ref[0]


def _mod_vectors(c8, w_mod, b_mod):
    depth, d, cols = w_mod.shape
    tn = cols // 4
    return pl.pallas_call(
        _mod_kernel,
        grid=(depth, cols // tn),
        in_specs=[
            pl.BlockSpec((8, d), lambda l, j: (0, 0)),
            pl.BlockSpec((1, d, tn), lambda l, j: (l, 0, j)),
            pl.BlockSpec((1, 1, tn), lambda l, j: (l, 0, j)),
        ],
        out_specs=pl.BlockSpec((1, 8, tn), lambda l, j: (l, 0, j)),
        out_shape=jax.ShapeDtypeStruct((depth, 8, cols), F32),
        compiler_params=pltpu.CompilerParams(
            dimension_semantics=("arbitrary", "arbitrary"), vmem_limit_bytes=VMEM_LIMIT),
        name="mod_vectors",
    )(c8, w_mod, b_mod.reshape(depth, 1, cols))


def _inproj_kernel(*refs):
    ns = INPROJ_SUB
    x_ref, g_ref, w_ref = refs[0], refs[1], refs[2]
    mod_refs = refs[3:3 + ns]
    cos_refs = refs[3 + ns:3 + 2 * ns]
    sin_refs = refs[3 + 2 * ns:3 + 3 * ns]
    z_ref, ab_ref, nq_ref, nk_ref, nv_ref, wq_ref, wk_ref, wv_ref, gate_ref = refs[3 + 3 * ns:]

    h = jnp.concatenate(
        [_modln(x_ref[j * TOK_TILE:(j + 1) * TOK_TILE, :], g_ref[...],
                mod_refs[j][0, 0:1, :], mod_refs[j][0, 1:2, :]).astype(BF16) for j in range(ns)], axis=0)

    def proj(c0, c1):
        return _dot(h, w_ref[:, c0:c1])

    a_h = proj(_C_CONV, _C_CONV + D_CONV)
    a_b = proj(_C_CONV + D_CONV, _C_CONV + 2 * D_CONV)
    a_c = proj(_C_CONV + 2 * D_CONV, _C_NQ)
    z_ref[...] = (a_c * a_h).astype(BF16)
    ab_ref[...] = a_b.astype(BF16)

    qk_scale = HEAD_DIM ** -0.5
    nq_ref[...] = (proj(_C_NQ, _C_NK) * qk_scale).astype(BF16)
    nk_ref[...] = proj(_C_NK, _C_NV).astype(BF16)
    nv_ref[...] = proj(_C_NV, _C_WQ).astype(BF16)

    cos = jnp.concatenate([r[...] for r in cos_refs], axis=0)
    sin = jnp.concatenate([r[...] for r in sin_refs], axis=0)
    cos3 = jnp.concatenate([cos] * (D_SW // LANES), axis=-1)
    sin3 = jnp.concatenate([sin] * (D_SW // LANES), axis=-1)

    def rope(v, cos_v, sin_v):
        quarter = HEAD_DIM // 4
        width = v.shape[-1]
        lane = lax.broadcasted_iota(jnp.int32, (1, width), 1)
        first = (lane % (2 * quarter)) < quarter
        swapped = jnp.where(first, pltpu.roll(v, width - quarter, axis=1), pltpu.roll(v, quarter, axis=1))
        return v * cos_v + swapped * sin_v

    wq_ref[...] = (rope(proj(_C_WQ, _C_WK), cos3, sin3) * qk_scale).astype(BF16)
    wk_ref[...] = rope(proj(_C_WK, _C_WV), cos, sin).astype(BF16)
    wv_ref[...] = proj(_C_WV, _C_GATE).astype(BF16)

    for j in range(3):
        c0 = _C_GATE + j * D_MODEL
        gate_ref[:, j * D_MODEL:(j + 1) * D_MODEL] = _sigmoid(proj(c0, c0 + D_MODEL)).astype(BF16)


def _inproj(x_all, mod8, g, w, cos_t, sin_t, n_lat):
    bsz, ltot, d = x_all.shape
    n_tiles = ltot // TOK_TILE
    ns = INPROJ_SUB
    tm = ns * TOK_TILE
    ntok = bsz * ltot
    assert ntok % tm == 0

    def tok_spec(width):
        return pl.BlockSpec((tm, width), lambda t: (t, 0))

    def mod_spec(j):
        def index(t):
            u = t * ns + j
            return (jnp.where(u % n_tiles < n_lat, u // n_tiles, bsz), 0, 0)
        return pl.BlockSpec((1, 8, d), index)

    def rope_spec(j):
        return pl.BlockSpec((TOK_TILE, LANES), lambda t: ((t * ns + j) % n_tiles, 0))

    widths = (D_CONV, D_CONV, D_NA, D_NA, D_NA, D_SW, D_KV, D_KV, 3 * D_MODEL)
    outs = pl.pallas_call(
        _inproj_kernel,
        grid=(ntok // tm,),
        in_specs=([tok_spec(d),
                   pl.BlockSpec((1, d), lambda t: (0, 0)),
                   pl.BlockSpec((d, _C_END), lambda t: (0, 0), pipeline_mode=pl.Buffered(1))]
                  + [mod_spec(j) for j in range(ns)]
                  + [rope_spec(j) for j in range(ns)]
                  + [rope_spec(j) for j in range(ns)]),
        out_specs=[tok_spec(wd) for wd in widths],
        out_shape=[jax.ShapeDtypeStruct((ntok, wd), BF16) for wd in widths],
        compiler_params=pltpu.CompilerParams(
            dimension_semantics=("arbitrary",), vmem_limit_bytes=VMEM_LIMIT),
        name="inproj",
    )(x_all.reshape(ntok, d), g, w, *([mod8] * ns), *([cos_t] * ns), *([sin_t] * ns))
    return [o.reshape(bsz, ltot, o.shape[-1]) for o in outs]


def _half_mask(half):
    lane = lax.broadcasted_iota(jnp.int32, (1, LANES), 1)
    return (lane < HEAD_DIM) if half == 0 else (lane >= HEAD_DIM)


def _softmax_pv(parts, sink=None):
    m = parts[0][0].max(axis=-1, keepdims=True)
    for s, _ in parts[1:]:
        m = jnp.maximum(m, s.max(axis=-1, keepdims=True))
    if sink is not None:
        m = jnp.maximum(m, sink)
    l = None
    o = None
    for s, v in parts:
        e = jnp.exp(s - m)
        ls = e.sum(axis=-1, keepdims=True)
        os_ = _dot(e.astype(BF16), v)
        l = ls if l is None else l + ls
        o = os_ if o is None else o + os_
    if sink is not None:
        l = l + jnp.exp(sink - m)
    return o * (1.0 / l)


def _na_kernel(q_ref, k_ref, v_ref, bias_ref, o_ref, *, n_blk, rows, seq):
    i = pl.program_id(1)
    ctx_len = k_ref.shape[1] - seq

    def pair_slices(p):
        return slice(p * LANES, (p + 1) * LANES)

    @pl.when(i < n_blk)
    def _latent():
        r0 = i * NA_ROWS
        ks = jnp.clip(r0 - WIN_R // 2, 0, rows - NA_KROWS)
        kstart = pl.multiple_of(ks * GRID_W, GRID_W)
        for p in range(H_NA // 2):
            ps = pair_slices(p)
            q2 = q_ref[0, :, ps]
            k2 = k_ref[0, pl.ds(kstart, NA_KEYS), ps]
            v2 = v_ref[0, pl.ds(kstart, NA_KEYS), ps]
            kc2 = k_ref[0, seq:seq + ctx_len, ps]
            vc2 = v_ref[0, seq:seq + ctx_len, ps]
            qm = jnp.concatenate([jnp.where(_half_mask(half), q2, jnp.zeros_like(q2)) for half in range(2)],
                                 axis=0)
            bias2 = jnp.concatenate([bias_ref[0, 2 * p], bias_ref[0, 2 * p + 1]], axis=0)
            o2 = _softmax_pv([(_dot_nt(qm, k2) + bias2, v2), (_dot_nt(qm, kc2), vc2)])
            o_ref[0, :, ps] = jnp.where(_half_mask(0), o2[:TOK_TILE], o2[TOK_TILE:]).astype(BF16)

    @pl.when(i == n_blk)
    def _context():
        for p in range(H_NA // 2):
            ps = pair_slices(p)
            q2 = q_ref[0, :, ps]
            kc2 = k_ref[0, seq:seq + ctx_len, ps]
            vc2 = v_ref[0, seq:seq + ctx_len, ps]
            outs = []
            for half in range(2):
                qm = jnp.where(_half_mask(half), q2, jnp.zeros_like(q2))
                outs.append(_softmax_pv([(_dot_nt(qm, kc2), vc2)]))
            o_ref[0, :, ps] = jnp.where(_half_mask(0), outs[0], outs[1]).astype(BF16)


def _na_attn(nq, nk, nv, bias, seq):
    bsz, ltot, _ = nq.shape
    rows = seq // GRID_W
    n_blk = seq // TOK_TILE
    n_steps = ltot // TOK_TILE

    def bias_map(b, i):
        return (jnp.where(i == 0, 0, jnp.where(i >= n_blk - 1, 2, 1)), 0, 0, 0)

    return pl.pallas_call(
        functools.partial(_na_kernel, n_blk=n_blk, rows=rows, seq=seq),
        grid=(bsz, n_steps),
        in_specs=[
            pl.BlockSpec((1, TOK_TILE, D_NA), lambda b, i: (b, i, 0)),
            pl.BlockSpec((1, ltot, D_NA), lambda b, i: (b, 0, 0)),
            pl.BlockSpec((1, ltot, D_NA), lambda b, i: (b, 0, 0)),
            pl.BlockSpec((1, H_NA, TOK_TILE, NA_KEYS), bias_map),
        ],
        out_specs=pl.BlockSpec((1, TOK_TILE, D_NA), lambda b, i: (b, i, 0)),
        out_shape=jax.ShapeDtypeStruct((bsz, ltot, D_NA), BF16),
        compiler_params=pltpu.CompilerParams(
            dimension_semantics=("arbitrary", "arbitrary"), vmem_limit_bytes=VMEM_LIMIT),
        name="na_attn",
    )(nq, nk, nv, bias)


def _sw_kernel(sink_ref, q_ref, k_ref, v_ref, o_ref, *, n_blk, seq):
    i = pl.program_id(1)
    ctx_len = k_ref.shape[1] - seq
    group = H_Q // H_KV

    def swap_halves(a):
        return jnp.concatenate([a[:, HEAD_DIM:], a[:, :HEAD_DIM]], axis=1)

    kv_nat = [k_ref[0, seq:seq + ctx_len, :], v_ref[0, seq:seq + ctx_len, :]]
    kv_swp = [swap_halves(a) for a in kv_nat]

    def heads_of_pair(p, win_nat, win_swp, valid):
        q2 = q_ref[0, :, p * LANES:(p + 1) * LANES]
        outs = []
        for half in range(2):
            head = 2 * p + half
            sink = sink_ref[head]
            natural = (head // group) == half
            kc2, vc2 = kv_nat if natural else kv_swp
            qm = jnp.where(_half_mask(half), q2, jnp.zeros_like(q2))
            parts = []
            if win_nat is not None:
                k2, v2 = win_nat if natural else win_swp
                parts.append((jnp.where(valid, _dot_nt(qm, k2), NEG_INF), v2))
            parts.append((_dot_nt(qm, kc2), vc2))
            outs.append(_softmax_pv(parts, sink=sink))
        o_ref[0, :, p * LANES:(p + 1) * LANES] = jnp.where(_half_mask(0), outs[0], outs[1]).astype(BF16)

    @pl.when(i < n_blk)
    def _latent():
        t0 = i * TOK_TILE
        kst = pl.multiple_of(jnp.clip(t0 - WINDOW, 0, seq - SW_KEYS), WINDOW)
        qpos = t0 + lax.broadcasted_iota(jnp.int32, (TOK_TILE, 1), 0)
        kpos = kst + lax.broadcasted_iota(jnp.int32, (1, SW_KEYS), 1)
        valid = jnp.abs(qpos - kpos) <= WINDOW
        win_nat = [k_ref[0, pl.ds(kst, SW_KEYS), :], v_ref[0, pl.ds(kst, SW_KEYS), :]]
        win_swp = [swap_halves(a) for a in win_nat]
        for p in range(H_Q // 2):
            heads_of_pair(p, win_nat, win_swp, valid)

    @pl.when(i == n_blk)
    def _context():
        for p in range(H_Q // 2):
            heads_of_pair(p, None, None, None)


def _sw_attn(sinks, wq, wk, wv, seq):
    bsz, ltot, _ = wq.shape
    n_blk = seq // TOK_TILE
    n_steps = ltot // TOK_TILE
    return pl.pallas_call(
        functools.partial(_sw_kernel, n_blk=n_blk, seq=seq),
        grid_spec=pltpu.PrefetchScalarGridSpec(
            num_scalar_prefetch=1,
            grid=(bsz, n_steps),
            in_specs=[
                pl.BlockSpec((1, TOK_TILE, D_SW), lambda b, i, s: (b, i, 0)),
                pl.BlockSpec((1, ltot, D_KV), lambda b, i, s: (b, 0, 0)),
                pl.BlockSpec((1, ltot, D_KV), lambda b, i, s: (b, 0, 0)),
            ],
            out_specs=pl.BlockSpec((1, TOK_TILE, D_SW), lambda b, i, s: (b, i, 0)),
        ),
        out_shape=jax.ShapeDtypeStruct((bsz, ltot, D_SW), BF16),
        compiler_params=pltpu.CompilerParams(
            dimension_semantics=("arbitrary", "arbitrary"), vmem_limit_bytes=VMEM_LIMIT),
        name="sw_attn",
    )(sinks, wq, wk, wv)


def _split_bf16(x):
    hi = x.astype(BF16)
    lo = (x - hi.astype(F32)).astype(BF16)
    return hi, lo


def _merge_kernel(x_ref, mod_ref, g_ref, z_ref, zp_ref, zn_ref, ab_ref, yb_ref, yc_ref, gate_ref,
                  convw_ref, wb_ref, wo_ref, wr_hi_ref, wr_lo_ref, br_ref,
                  xo_ref, h2_ref, rt_ref, *, n_lat):
    i = pl.program_id(1)
    n_tiles = pl.num_programs(1)
    tm = x_ref.shape[1]

    z = z_ref[0].astype(F32)
    halo = zp_ref.shape[1]
    first = (i == 0) | (i == n_lat)
    last = (i == n_lat - 1) | (i == n_tiles - 1)
    prev_row = jnp.where(first, 0.0, zp_ref[0, halo - 1:halo, :].astype(F32))
    next_row = jnp.where(last, 0.0, zn_ref[0, 0:1, :].astype(F32))
    row = lax.broadcasted_iota(jnp.int32, (tm, 1), 0)
    z_prev = jnp.where(row == 0, prev_row, pltpu.roll(z, 1, axis=0))
    z_next = jnp.where(row == tm - 1, next_row, pltpu.roll(z, tm - 1, axis=0))
    cw = convw_ref[...]
    y_a = ab_ref[0].astype(F32) * (z_prev * cw[0:1] + z * cw[1:2] + z_next * cw[2:3])

    o1 = D_CONV
    o2 = D_CONV + D_NA
    g_a = gate_ref[0, :, 0:D_MODEL].astype(F32)
    g_b = gate_ref[0, :, D_MODEL:2 * D_MODEL].astype(F32)
    g_c = gate_ref[0, :, 2 * D_MODEL:3 * D_MODEL].astype(F32)
    m = (g_a * _dot(y_a.astype(BF16), wb_ref[0:o1, :])
         + g_b * _dot(yb_ref[0], wb_ref[o1:o2, :])
         + g_c * _dot(yc_ref[0], wb_ref[o2:, :]))
    y = _dot(m.astype(BF16), wo_ref[...])
    x_new = x_ref[0] + mod_ref[0, 2:3, :] * y
    xo_ref[0] = x_new

    h2 = _modln(x_new, g_ref[...], mod_ref[0, 3:4, :], mod_ref[0, 4:5, :])
    _store_row_tiles(h2_ref, h2)

    h_hi, h_lo = _split_bf16(h2)
    lg = (_dot(h_hi, wr_hi_ref[...]) + _dot(h_hi, wr_lo_ref[...]) + _dot(h_lo, wr_hi_ref[...])
          + br_ref[...])
    lane = lax.broadcasted_iota(jnp.int32, lg.shape, 1)
    big = jnp.int32(LANES)

    def masked_argmax(vals, mask):
        v = jnp.where(mask, vals, -jnp.inf)
        vmax = v.max(axis=-1, keepdims=True)
        idx = jnp.where(mask & (v == vmax), lane, big).min(axis=-1, keepdims=True)
        return vmax, idx

    is_grp = lane < N_GROUPS
    g_max, g_idx = masked_argmax(lg, is_grp)
    pg_top = 1.0 / jnp.where(is_grp, jnp.exp(lg - g_max), 0.0).sum(axis=-1, keepdims=True)
    e_lo = 8 + g_idx * EXP_PER_GROUP
    in_grp = (lane >= e_lo) & (lane < e_lo + EXP_PER_GROUP)
    v1, i1 = masked_argmax(lg, in_grp)
    v2, i2 = masked_argmax(lg, in_grp & (lane != i1))
    e21 = jnp.exp(v2 - v1)
    w1 = pg_top / (1.0 + e21)
    w2 = pg_top * e21 / (1.0 + e21)
    rt = jnp.where(lane == 0, (i1 - 8).astype(F32),
                   jnp.where(lane == 1, (i2 - 8).astype(F32),
                             jnp.where(lane == 2, w1, jnp.where(lane == 3, w2, 0.0))))
    rt_ref[0] = rt


def _merge(x_all, mod8, g, z, ab, y_b, y_c, gates, conv_w, wb, wo, wr_hi, wr_lo, br, n_lat):
    bsz, ltot, d = x_all.shape
    n_tiles = ltot // TOK_TILE
    tm = TOK_TILE
    halo = 16
    hb = tm // halo
    n_halo = ltot // halo

    def tok_spec(width):
        return pl.BlockSpec((1, tm, width), lambda b, i: (b, i, 0))

    def mod_map(b, i):
        return (jnp.where(i < n_lat, b, bsz), 0, 0)

    def const2(shape):
        return pl.BlockSpec(shape, lambda b, i: (0, 0))

    return pl.pallas_call(
        functools.partial(_merge_kernel, n_lat=n_lat),
        grid=(bsz, n_tiles),
        in_specs=[
            tok_spec(d),
            pl.BlockSpec((1, 8, d), mod_map),
            const2((1, d)),
            tok_spec(D_CONV),
            pl.BlockSpec((1, halo, D_CONV), lambda b, i: (b, jnp.maximum(i * hb - 1, 0), 0)),
            pl.BlockSpec((1, halo, D_CONV), lambda b, i: (b, jnp.minimum((i + 1) * hb, n_halo - 1), 0)),
            tok_spec(D_CONV),
            tok_spec(D_NA),
            tok_spec(D_SW),
            tok_spec(3 * d),
            const2((8, D_CONV)),
            const2((d, d)),
            const2((d, d)),
            const2((d, LANES)),
            const2((d, LANES)),
            const2((1, LANES)),
        ],
        out_specs=[tok_spec(d),
                   pl.BlockSpec((tm * ROW_SUB, LANES), lambda b, i: (b * n_tiles + i, 0)),
                   tok_spec(LANES)],
        out_shape=[jax.ShapeDtypeStruct((bsz, ltot, d), F32),
                   jax.ShapeDtypeStruct((bsz * ltot * ROW_SUB, LANES), F32),
                   jax.ShapeDtypeStruct((bsz, ltot, LANES), F32)],
        compiler_params=pltpu.CompilerParams(
            dimension_semantics=("arbitrary", "arbitrary"), vmem_limit_bytes=VMEM_LIMIT),
        name="merge",
    )(x_all, mod8, g, z, z, z, ab, y_b, y_c, gates, conv_w, wb, wo, wr_hi, wr_lo, br)


def _moe_kernel(blk_e_ref, nused_ref, base_ref, cnt_ref, order_ref, h2_hbm, wg_ref, wu_ref, wd_ref,
                yd_ref, wg_s, wu_s, wd_s, xbuf, xsem):
    i = pl.program_id(0)
    nused = nused_ref[0]
    blk_rows = MOE_BLK * ROW_SUB

    def start_gather(j):
        slot = j % 2
        base = base_ref[j]
        last = cnt_ref[j] - 1

        def body(r0, carry):
            for u in range(DMA_UNROLL):
                r = r0 * DMA_UNROLL + u
                src = pl.multiple_of(order_ref[base + jnp.minimum(r, last)], ROW_SUB)
                dst = pl.multiple_of(r * ROW_SUB, ROW_SUB)
                pltpu.make_async_copy(h2_hbm.at[pl.ds(src, ROW_SUB)], xbuf.at[slot, pl.ds(dst, ROW_SUB)],
                                      xsem.at[slot]).start(priority=u % 2)
            return carry
        lax.fori_loop(0, MOE_BLK // DMA_UNROLL, body, 0)

    @pl.when((i == 0) & (nused > 0))
    def _prime():
        start_gather(0)

    @pl.when(i + 1 < nused)
    def _prefetch():
        start_gather(i + 1)

    @pl.when((i == 0) | (blk_e_ref[i] != blk_e_ref[jnp.maximum(i - 1, 0)]))
    def _cast_weights():
        wg_s[...] = wg_ref[0].astype(BF16)
        wu_s[...] = wu_ref[0].astype(BF16)
        wd_s[...] = wd_ref[0].astype(BF16)

    @pl.when(i < nused)
    def _compute():
        slot = i % 2
        pltpu.make_async_copy(h2_hbm.at[pl.ds(0, blk_rows)], xbuf.at[slot], xsem.at[slot]).wait()
        xb = _load_row_tiles(xbuf.at[slot], MOE_BLK).astype(BF16)
        gate = _dot(xb, wg_s[...])
        up = _dot(xb, wu_s[...])
        hid = (gate * _sigmoid(gate)) * up
        _store_row_tiles(yd_ref, _dot(hid.astype(BF16), wd_s[...]))

    @pl.when(i >= nused)
    def _unused():
        yd_ref[...] = jnp.zeros_like(yd_ref)


def _moe(blk_e, nused, blk_base, blk_cnt, order8, h2t, wg, wu, wd):
    nb = blk_e.shape[0]
    d = wg.shape[1]
    blk_rows = MOE_BLK * ROW_SUB

    def w_spec(shape):
        return pl.BlockSpec(shape, lambda i, be, nu, bb, bc, od: (be[i], 0, 0))

    return pl.pallas_call(
        _moe_kernel,
        grid_spec=pltpu.PrefetchScalarGridSpec(
            num_scalar_prefetch=5,
            grid=(nb,),
            in_specs=[
                pl.BlockSpec(memory_space=pl.ANY),
                w_spec((1, d, D_EXPERT)),
                w_spec((1, d, D_EXPERT)),
                w_spec((1, D_EXPERT, d)),
            ],
            out_specs=pl.BlockSpec((blk_rows, LANES), lambda i, be, nu, bb, bc, od: (i, 0)),
            scratch_shapes=[pltpu.VMEM((d, D_EXPERT), BF16), pltpu.VMEM((d, D_EXPERT), BF16),
                            pltpu.VMEM((D_EXPERT, d), BF16),
                            pltpu.VMEM((2, blk_rows, LANES), F32),
                            pltpu.SemaphoreType.DMA((2,))],
        ),
        out_shape=jax.ShapeDtypeStruct((nb * blk_rows, LANES), F32),
        compiler_params=pltpu.CompilerParams(
            dimension_semantics=("arbitrary",), vmem_limit_bytes=VMEM_LIMIT),
        name="moe_experts",
    )(blk_e, nused, blk_base, blk_cnt, order8, h2t, wg, wu, wd)


def _combine_kernel(dest_ref, x_ref, mod_ref, g_ref, rt_ref, yd_hbm, o_ref, buf, sem, *, final, tile_of):
    s = pl.program_id(0)
    n_steps = pl.num_programs(0)
    tm = x_ref.shape[0]
    slot = s % 2

    def start_gather(step, dst_slot):
        base = tile_of(step) * tm

        def body(r, carry):
            for k in range(2):
                src = pl.multiple_of(dest_ref[(base + r) * 2 + k], ROW_SUB)
                dst = pl.multiple_of(r * ROW_SUB, ROW_SUB)
                pltpu.make_async_copy(yd_hbm.at[pl.ds(src, ROW_SUB)], buf.at[dst_slot, k, pl.ds(dst, ROW_SUB)],
                                      sem.at[dst_slot]).start(priority=k)
            return carry
        lax.fori_loop(0, tm, body, 0, unroll=DMA_UNROLL)

    @pl.when(s == 0)
    def _prime():
        start_gather(0, 0)

    @pl.when(s + 1 < n_steps)
    def _prefetch():
        start_gather(s + 1, 1 - slot)

    for k in range(2):
        pltpu.make_async_copy(yd_hbm.at[pl.ds(0, tm * ROW_SUB)], buf.at[slot, k], sem.at[slot]).wait()
    w0 = rt_ref[:, 2:3]
    w1 = rt_ref[:, 3:4]
    y = w0 * _load_row_tiles(buf.at[slot, 0], tm) + w1 * _load_row_tiles(buf.at[slot, 1], tm)
    x_new = x_ref[...] + mod_ref[0, 5:6, :] * y
    if final:
        ms = jnp.mean(x_new * x_new, axis=-1, keepdims=True)
        x_new = x_new * lax.rsqrt(ms + EPS) * g_ref[...]
    o_ref[...] = x_new


def _combine(dest, x_flat, mod8, g_final, rt, yd, n_lat, n_tiles, final):
    ntok, d = x_flat.shape
    tm = TOK_TILE
    bsz = ntok // (tm * n_tiles)
    n_per = n_lat if final else n_tiles
    n_steps = bsz * n_per

    def tile_of(step):
        return (step // n_per) * n_tiles + step % n_per

    def mod_map(s, dref):
        return (jnp.where(s % n_per < n_lat, s // n_per, bsz), 0, 0)

    return pl.pallas_call(
        functools.partial(_combine_kernel, final=final, tile_of=tile_of),
        grid_spec=pltpu.PrefetchScalarGridSpec(
            num_scalar_prefetch=1,
            grid=(n_steps,),
            in_specs=[
                pl.BlockSpec((tm, d), lambda s, dref: (tile_of(s), 0)),
                pl.BlockSpec((1, 8, d), mod_map),
                pl.BlockSpec((1, d), lambda s, dref: (0, 0)),
                pl.BlockSpec((tm, LANES), lambda s, dref: (tile_of(s), 0)),
                pl.BlockSpec(memory_space=pl.ANY),
            ],
            out_specs=pl.BlockSpec((tm, d), lambda s, dref: (s, 0)),
            scratch_shapes=[pltpu.VMEM((2, 2, tm * ROW_SUB, LANES), F32), pltpu.SemaphoreType.DMA((2,))],
        ),
        out_shape=jax.ShapeDtypeStruct((n_steps * tm, d), F32),
        compiler_params=pltpu.CompilerParams(
            dimension_semantics=("arbitrary",), vmem_limit_bytes=VMEM_LIMIT),
        name="combine",
    )(dest, x_flat, mod8, g_final, rt, yd)


def _rope_tables(seq, ctx_len):
    quarter = HEAD_DIM // 4
    rows = seq // GRID_W
    inv = ROPE_BASE ** (-jnp.arange(quarter, dtype=F32) / quarter)
    ang_r = jnp.arange(rows, dtype=F32)[:, None] * inv[None, :]
    ang_c = jnp.arange(GRID_W, dtype=F32)[:, None] * inv[None, :]

    def by_row(a):
        return jnp.broadcast_to(a[:, None, :], (rows, GRID_W, quarter)).reshape(seq, quarter)

    def by_col(a):
        return jnp.broadcast_to(a[None, :, :], (rows, GRID_W, quarter)).reshape(seq, quarter)

    cos_r, sin_r = by_row(jnp.cos(ang_r)), by_row(jnp.sin(ang_r))
    cos_c, sin_c = by_col(jnp.cos(ang_c)), by_col(jnp.sin(ang_c))
    cos = jnp.concatenate([cos_r, cos_r, cos_c, cos_c], axis=-1)
    sin = jnp.concatenate([-sin_r, sin_r, -sin_c, sin_c], axis=-1)
    cos = jnp.concatenate([cos, jnp.ones((ctx_len, HEAD_DIM), F32)], axis=0)
    sin = jnp.concatenate([sin, jnp.zeros((ctx_len, HEAD_DIM), F32)], axis=0)
    return jnp.concatenate([cos, cos], axis=-1), jnp.concatenate([sin, sin], axis=-1)


def _na_bias_tables(rpb, rows):
    qc = np.arange(GRID_W)
    kc = np.arange(GRID_W)
    cstart = np.clip(qc - WIN_C // 2, 0, GRID_W - WIN_C)
    rel_c = np.clip(kc[None, :] - qc[:, None] + WIN_C - 1, 0, 2 * WIN_C - 2)
    col_ok = (kc[None, :] >= cstart[:, None]) & (kc[None, :] < cstart[:, None] + WIN_C)
    t = jnp.where(col_ok[None, None], rpb[:, :, rel_c], NEG_INF)
    masked = jnp.full((rpb.shape[0], GRID_W, GRID_W), NEG_INF, F32)
    n_blk = rows // NA_ROWS
    cases = []
    for blk in (0, min(1, n_blk - 1), n_blk - 1):
        r0 = blk * NA_ROWS
        ks = int(np.clip(r0 - WIN_R // 2, 0, rows - NA_KROWS))
        q_rows = []
        for qr in range(NA_ROWS):
            r = r0 + qr
            rs = int(np.clip(r - WIN_R // 2, 0, rows - WIN_R))
            tiles = []
            for a in range(NA_KROWS):
                kr = ks + a
                tiles.append(t[:, kr - r + WIN_R - 1] if rs <= kr < rs + WIN_R else masked)
            q_rows.append(jnp.concatenate(tiles, axis=-1))
        cases.append(jnp.concatenate(q_rows, axis=-2))
    return jnp.stack(cases, axis=0).astype(F32)


def _routing_plan(rt, n_blocks):
    ntok = rt.shape[0]
    eid = rt[:, 0:2].astype(jnp.int32).reshape(-1)
    onehot = (eid[:, None] == jnp.arange(N_EXPERTS)[None, :]).astype(jnp.int32)
    csum = jnp.cumsum(onehot, axis=0)
    counts = csum[-1]
    rank = jnp.take_along_axis(csum, eid[:, None], axis=1)[:, 0] - 1
    padded = ((counts + MOE_BLK - 1) // MOE_BLK) * MOE_BLK
    pad_end = jnp.cumsum(padded)
    pad_start = pad_end - padded
    dest8 = ((pad_start[eid] + rank) * ROW_SUB).astype(jnp.int32)
    blk_start = jnp.arange(n_blocks, dtype=jnp.int32) * MOE_BLK
    blk_e = jnp.clip(jnp.sum(pad_end[None, :] <= blk_start[:, None], axis=1), 0, N_EXPERTS - 1).astype(jnp.int32)
    nused = (pad_end[-1] // MOE_BLK).astype(jnp.int32).reshape(1)
    n_assign = eid.shape[0]
    shift = int(n_assign - 1).bit_length()
    keys = jnp.sort((eid << shift) | jnp.arange(n_assign, dtype=jnp.int32))
    order8 = (((keys & ((1 << shift) - 1)) >> 1) * ROW_SUB).astype(jnp.int32)
    seg_start = jnp.cumsum(counts) - counts
    in_seg = blk_start - pad_start[blk_e]
    blk_base = jnp.clip(seg_start[blk_e] + in_seg, 0, n_assign - 1).astype(jnp.int32)
    blk_cnt = jnp.clip(counts[blk_e] - in_seg, 1, MOE_BLK).astype(jnp.int32)
    return blk_e, nused, dest8, blk_base, blk_cnt, order8


def kernel(x, c, ctx, c_ctx, w_mod, b_mod, g_attn, g_ffn, w_in, conv_w, rpb, sinks, w_branch, w_o,
           w_rg, b_rg, w_re, b_re, w_e_gate, w_e_up, w_e_down, g_final):
    bsz, seq, d = x.shape
    ctx_len = ctx.shape[1]
    depth = w_in.shape[0]
    ltot = seq + ctx_len
    n_lat = seq // TOK_TILE
    n_tiles = ltot // TOK_TILE
    rows = seq // GRID_W
    ntok = bsz * ltot
    n_assign = ntok * 2
    n_blocks = (n_assign + N_EXPERTS * (MOE_BLK - 1) + MOE_BLK - 1) // MOE_BLK

    x_all = jnp.concatenate([x, ctx], axis=1)
    cos_t, sin_t = _rope_tables(seq, ctx_len)
    assert bsz + 1 <= 8
    c8 = jnp.pad(jnp.concatenate([c, c_ctx[None, :]], axis=0), ((0, 7 - bsz), (0, 0)))
    mod_all = _mod_vectors(c8, w_mod, b_mod)
    w_in_bf = w_in.astype(BF16)
    w_branch_bf = w_branch.astype(BF16)
    w_o_bf = w_o.astype(BF16)

    for l in range(depth):
        mod8 = jnp.pad(mod_all[l, :bsz + 1].reshape(bsz + 1, 6, d), ((0, 0), (0, 2), (0, 0)))
        w_l = w_in_bf[l]
        bias = _na_bias_tables(rpb[l], rows)
        wb = w_branch_bf[l]
        pad_g = jnp.zeros((d, 8 - N_GROUPS), F32)
        pad_e = jnp.zeros((d, LANES - 8 - N_EXPERTS), F32)
        wr = jnp.concatenate([w_rg[l], pad_g, w_re[l], pad_e], axis=1)
        wr_hi = wr.astype(BF16)
        wr_lo = (wr - wr_hi.astype(F32)).astype(BF16)
        br = jnp.concatenate([b_rg[l], pad_g[0], b_re[l], pad_e[0]])[None, :]
        convw8 = jnp.pad(conv_w[l], ((0, 8 - CONV_W), (0, 0)))

        z, ab, nq, nk, nv, wq, wk, wv, gates = _inproj(
            x_all, mod8, g_attn[l][None, :], w_l, cos_t, sin_t, n_lat)
        y_b = _na_attn(nq, nk, nv, bias, seq)
        y_c = _sw_attn(sinks[l], wq, wk, wv, seq)
        x_mid, h2t, rt = _merge(x_all, mod8, g_ffn[l][None, :], z, ab, y_b, y_c, gates, convw8, wb,
                                w_o_bf[l], wr_hi, wr_lo, br, n_lat)
        rt = rt.reshape(ntok, LANES)
        blk_e, nused, dest8, blk_base, blk_cnt, order8 = _routing_plan(rt, n_blocks)
        yd = _moe(blk_e, nused, blk_base, blk_cnt, order8, h2t, w_e_gate[l], w_e_up[l], w_e_down[l])
        final = l == depth - 1
        x_flat = _combine(dest8, x_mid.reshape(ntok, d), mod8, g_final[None, :], rt, yd, n_lat, n_tiles,
                          final)
        if not final:
            x_all = x_flat.reshape(bsz, ltot, d)
    return x_flat.reshape(bsz, seq, d)
```

```python
import functools

import numpy as np
import jax
import jax.numpy as jnp
from jax import lax
from jax.experimental import pallas as pl
from jax.experimental.pallas import tpu as pltpu

F32 = jnp.float32
BF16 = jnp.bfloat16

D_MODEL = 1024
GRID_W = 64
HEAD_DIM = 64
D_CONV = 256
CONV_W = 3
H_NA = 6
WIN_R = 8
WIN_C = 16
H_Q = 6
H_KV = 2
WINDOW = 128
ROPE_BASE = 10000.0
D_NA = H_NA * HEAD_DIM
D_SW = H_Q * HEAD_DIM
D_KV = H_KV * HEAD_DIM
N_GROUPS = 4
EXP_PER_GROUP = 8
N_EXPERTS = N_GROUPS * EXP_PER_GROUP
D_EXPERT = 256
MOE_BLK = 256
EPS = 1e-6
NEG_INF = -1e30

LANES = 128
ROW_SUB = D_MODEL // LANES
TOK_TILE = 256
INPROJ_SUB = 2
NA_ROWS = TOK_TILE // GRID_W
NA_KROWS = NA_ROWS + WIN_R - 1
NA_KEYS = NA_KROWS * GRID_W
SW_KEYS = TOK_TILE + 2 * WINDOW
VMEM_LIMIT = 56 * 1024 * 1024
DMA_UNROLL = 8
MOE_RING = 3

_C_CONV = 0
_C_NQ = _C_CONV + 3 * D_CONV
_C_NK = _C_NQ + D_NA
_C_NV = _C_NK + D_NA
_C_WQ = _C_NV + D_NA
_C_WK = _C_WQ + D_SW
_C_WV = _C_WK + D_KV
_C_GATE = _C_WV + D_KV
_C_END = _C_GATE + 3 * D_MODEL

_SW_HEAD_ORDER = (0, 3, 1, 4, 2, 5)


def _dot(a, b):
    return jnp.dot(a, b, preferred_element_type=F32)


def _dot_nt(a, b):
    return lax.dot_general(a, b, (((1,), (1,)), ((), ())), preferred_element_type=F32)


def _sigmoid(x):
    return 1.0 / (1.0 + jnp.exp(-x))


def _modln(x, g, shift, scale):
    ms = jnp.mean(x * x, axis=-1, keepdims=True)
    return (x * lax.rsqrt(ms + EPS)) * g * (1.0 + scale) + shift


def _store_row_tiles(ref, val):
    n = val.shape[0]
    for s in range(ROW_SUB):
        ref[pl.ds(s, n, stride=ROW_SUB), :] = val[:, s * LANES:(s + 1) * LANES]


def _load_row_tiles(ref, n):
    return jnp.concatenate([ref[pl.ds(s, n, stride=ROW_SUB), :] for s in range(ROW_SUB)], axis=-1)


def _mod_kernel(c_ref, w_ref, b_ref, o_ref):
    cv = c_ref[...]
    sc = (cv * _sigmoid(cv)).astype(BF16)
    o_ref[0] = _dot(sc, w_ref[0].astype(BF16)) + b_ref[0]


def _mod_vectors(c8, w_mod, b_mod):
    depth, d, cols = w_mod.shape
    tn = cols // 4
    return pl.pallas_call(
        _mod_kernel,
        grid=(depth, cols // tn),
        in_specs=[
            pl.BlockSpec((8, d), lambda l, j: (0, 0)),
            pl.BlockSpec((1, d, tn), lambda l, j: (l, 0, j)),
            pl.BlockSpec((1, 1, tn), lambda l, j: (l, 0, j)),
        ],
        out_specs=pl.BlockSpec((1, 8, tn), lambda l, j: (l, 0, j)),
        out_shape=jax.ShapeDtypeStruct((depth, 8, cols), F32),
        compiler_params=pltpu.CompilerParams(
            dimension_semantics=("arbitrary", "arbitrary"), vmem_limit_bytes=VMEM_LIMIT),
        name="mod_vectors",
    )(c8, w_mod, b_mod.reshape(depth, 1, cols))


def _inproj_kernel(*refs):
    ns = INPROJ_SUB
    x_ref, g_ref, w_ref = refs[0], refs[1], refs[2]
    mod_refs = refs[3:3 + ns]
    cos_refs = refs[3 + ns:3 + 2 * ns]
    sin_refs = refs[3 + 2 * ns:3 + 3 * ns]
    z_ref, ab_ref, nq_ref, nk_ref, nv_ref, wq_ref, wk_ref, wv_ref, gate_ref = refs[3 + 3 * ns:]

    h = jnp.concatenate(
        [_modln(x_ref[j * TOK_TILE:(j + 1) * TOK_TILE, :], g_ref[...],
                mod_refs[j][0, 0:1, :], mod_refs[j][0, 1:2, :]).astype(BF16) for j in range(ns)], axis=0)

    def proj(c0, c1):
        return _dot(h, w_ref[:, c0:c1])

    a_h = proj(_C_CONV, _C_CONV + D_CONV)
    a_b = proj(_C_CONV + D_CONV, _C_CONV + 2 * D_CONV)
    a_c = proj(_C_CONV + 2 * D_CONV, _C_NQ)
    z_ref[...] = (a_c * a_h).astype(BF16)
    ab_ref[...] = a_b.astype(BF16)

    qk_scale = HEAD_DIM ** -0.5
    nq_ref[...] = (proj(_C_NQ, _C_NK) * qk_scale).astype(BF16)
    nk_ref[...] = proj(_C_NK, _C_NV).astype(BF16)
    nv_ref[...] = proj(_C_NV, _C_WQ).astype(BF16)

    cos = jnp.concatenate([r[...] for r in cos_refs], axis=0)
    sin = jnp.concatenate([r[...] for r in sin_refs], axis=0)
    cos3 = jnp.concatenate([cos] * (D_SW // LANES), axis=-1)
    sin3 = jnp.concatenate([sin] * (D_SW // LANES), axis=-1)

    def rope(v, cos_v, sin_v):
        quarter = HEAD_DIM // 4
        width = v.shape[-1]
        lane = lax.broadcasted_iota(jnp.int32, (1, width), 1)
        first = (lane % (2 * quarter)) < quarter
        swapped = jnp.where(first, pltpu.roll(v, width - quarter, axis=1), pltpu.roll(v, quarter, axis=1))
        return v * cos_v + swapped * sin_v

    wq = rope(proj(_C_WQ, _C_WK), cos3, sin3) * qk_scale
    wq = jnp.concatenate([wq[:, h * HEAD_DIM:(h + 1) * HEAD_DIM] for h in _SW_HEAD_ORDER], axis=1)
    wq_ref[...] = wq.astype(BF16)
    wk_ref[...] = rope(proj(_C_WK, _C_WV), cos, sin).astype(BF16)
    wv_ref[...] = proj(_C_WV, _C_GATE).astype(BF16)

    for j in range(3):
        c0 = _C_GATE + j * D_MODEL
        gate_ref[:, j * D_MODEL:(j + 1) * D_MODEL] = _sigmoid(proj(c0, c0 + D_MODEL)).astype(BF16)


def _inproj(x_all, mod8, g, w, cos_t, sin_t, n_lat):
    bsz, ltot, d = x_all.shape
    n_tiles = ltot // TOK_TILE
    ns = INPROJ_SUB
    tm = ns * TOK_TILE
    ntok = bsz * ltot
    assert ntok % tm == 0

    def tok_spec(width):
        return pl.BlockSpec((tm, width), lambda t: (t, 0))

    def mod_spec(j):
        def index(t):
            u = t * ns + j
            return (jnp.where(u % n_tiles < n_lat, u // n_tiles, bsz), 0, 0)
        return pl.BlockSpec((1, 8, d), index)

    def rope_spec(j):
        return pl.BlockSpec((TOK_TILE, LANES), lambda t: ((t * ns + j) % n_tiles, 0))

    widths = (D_CONV, D_CONV, D_NA, D_NA, D_NA, D_SW, D_KV, D_KV, 3 * D_MODEL)
    outs = pl.pallas_call(
        _inproj_kernel,
        grid=(ntok // tm,),
        in_specs=([tok_spec(d),
                   pl.BlockSpec((1, d), lambda t: (0, 0)),
                   pl.BlockSpec((d, _C_END), lambda t: (0, 0), pipeline_mode=pl.Buffered(1))]
                  + [mod_spec(j) for j in range(ns)]
                  + [rope_spec(j) for j in range(ns)]
                  + [rope_spec(j) for j in range(ns)]),
        out_specs=[tok_spec(wd) for wd in widths],
        out_shape=[jax.ShapeDtypeStruct((ntok, wd), BF16) for wd in widths],
        compiler_params=pltpu.CompilerParams(
            dimension_semantics=("arbitrary",), vmem_limit_bytes=VMEM_LIMIT),
        name="inproj",
    )(x_all.reshape(ntok, d), g, w, *([mod8] * ns), *([cos_t] * ns), *([sin_t] * ns))
    return [o.reshape(bsz, ltot, o.shape[-1]) for o in outs]


def _half_mask(half):
    lane = lax.broadcasted_iota(jnp.int32, (1, LANES), 1)
    return (lane < HEAD_DIM) if half == 0 else (lane >= HEAD_DIM)


def _softmax_pv(parts, sink=None):
    m = parts[0][0].max(axis=-1, keepdims=True)
    for s, _ in parts[1:]:
        m = jnp.maximum(m, s.max(axis=-1, keepdims=True))
    if sink is not None:
        m = jnp.maximum(m, sink)
    l = None
    o = None
    for s, v in parts:
        e = jnp.exp(s - m)
        ls = e.sum(axis=-1, keepdims=True)
        os_ = _dot(e.astype(BF16), v)
        l = ls if l is None else l + ls
        o = os_ if o is None else o + os_
    if sink is not None:
        l = l + jnp.exp(sink - m)
    return o * (1.0 / l)


def _na_kernel(q_ref, k_ref, v_ref, bias_ref, o_ref, *, n_blk, rows, seq):
    i = pl.program_id(1)
    ctx_len = k_ref.shape[1] - seq

    def pair_slices(p):
        return slice(p * LANES, (p + 1) * LANES)

    @pl.when(i < n_blk)
    def _latent():
        r0 = i * NA_ROWS
        ks = jnp.clip(r0 - WIN_R // 2, 0, rows - NA_KROWS)
        kstart = pl.multiple_of(ks * GRID_W, GRID_W)
        for p in range(H_NA // 2):
            ps = pair_slices(p)
            q2 = q_ref[0, :, ps]
            k2 = k_ref[0, pl.ds(kstart, NA_KEYS), ps]
            v2 = v_ref[0, pl.ds(kstart, NA_KEYS), ps]
            kc2 = k_ref[0, seq:seq + ctx_len, ps]
            vc2 = v_ref[0, seq:seq + ctx_len, ps]
            qm = jnp.concatenate([jnp.where(_half_mask(half), q2, jnp.zeros_like(q2)) for half in range(2)],
                                 axis=0)
            bias2 = jnp.concatenate([bias_ref[0, 2 * p], bias_ref[0, 2 * p + 1]], axis=0)
            o2 = _softmax_pv([(_dot_nt(qm, k2) + bias2, v2), (_dot_nt(qm, kc2), vc2)])
            o_ref[0, :, ps] = jnp.where(_half_mask(0), o2[:TOK_TILE], o2[TOK_TILE:]).astype(BF16)

    @pl.when(i == n_blk)
    def _context():
        for p in range(H_NA // 2):
            ps = pair_slices(p)
            q2 = q_ref[0, :, ps]
            kc2 = k_ref[0, seq:seq + ctx_len, ps]
            vc2 = v_ref[0, seq:seq + ctx_len, ps]
            outs = []
            for half in range(2):
                qm = jnp.where(_half_mask(half), q2, jnp.zeros_like(q2))
                outs.append(_softmax_pv([(_dot_nt(qm, kc2), vc2)]))
            o_ref[0, :, ps] = jnp.where(_half_mask(0), outs[0], outs[1]).astype(BF16)


def _na_attn(nq, nk, nv, bias, seq):
    bsz, ltot, _ = nq.shape
    rows = seq // GRID_W
    n_blk = seq // TOK_TILE
    n_steps = ltot // TOK_TILE

    def bias_map(b, i):
        return (jnp.where(i == 0, 0, jnp.where(i >= n_blk - 1, 2, 1)), 0, 0, 0)

    return pl.pallas_call(
        functools.partial(_na_kernel, n_blk=n_blk, rows=rows, seq=seq),
        grid=(bsz, n_steps),
        in_specs=[
            pl.BlockSpec((1, TOK_TILE, D_NA), lambda b, i: (b, i, 0)),
            pl.BlockSpec((1, ltot, D_NA), lambda b, i: (b, 0, 0)),
            pl.BlockSpec((1, ltot, D_NA), lambda b, i: (b, 0, 0)),
            pl.BlockSpec((1, H_NA, TOK_TILE, NA_KEYS), bias_map),
        ],
        out_specs=pl.BlockSpec((1, TOK_TILE, D_NA), lambda b, i: (b, i, 0)),
        out_shape=jax.ShapeDtypeStruct((bsz, ltot, D_NA), BF16),
        compiler_params=pltpu.CompilerParams(
            dimension_semantics=("arbitrary", "arbitrary"), vmem_limit_bytes=VMEM_LIMIT),
        name="na_attn",
    )(nq, nk, nv, bias)


def _sw_kernel(sink_ref, q_ref, k_ref, v_ref, o_ref, *, n_blk, seq):
    i = pl.program_id(1)
    ctx_len = k_ref.shape[1] - seq
    kc2 = k_ref[0, seq:seq + ctx_len, :]
    vc2 = v_ref[0, seq:seq + ctx_len, :]

    def heads_of_pair(p, window):
        q2 = q_ref[0, :, p * LANES:(p + 1) * LANES]
        outs = []
        for half in range(2):
            sink = sink_ref[half * (H_Q // H_KV) + p]
            qm = jnp.where(_half_mask(half), q2, jnp.zeros_like(q2))
            parts = []
            if window is not None:
                k2, v2, valid = window
                parts.append((jnp.where(valid, _dot_nt(qm, k2), NEG_INF), v2))
            parts.append((_dot_nt(qm, kc2), vc2))
            outs.append(_softmax_pv(parts, sink=sink))
        o_ref[0, :, p * LANES:(p + 1) * LANES] = jnp.where(_half_mask(0), outs[0], outs[1]).astype(BF16)

    @pl.when(i < n_blk)
    def _latent():
        t0 = i * TOK_TILE
        kst = pl.multiple_of(jnp.clip(t0 - WINDOW, 0, seq - SW_KEYS), WINDOW)
        qpos = t0 + lax.broadcasted_iota(jnp.int32, (TOK_TILE, 1), 0)
        kpos = kst + lax.broadcasted_iota(jnp.int32, (1, SW_KEYS), 1)
        valid = jnp.abs(qpos - kpos) <= WINDOW
        window = (k_ref[0, pl.ds(kst, SW_KEYS), :], v_ref[0, pl.ds(kst, SW_KEYS), :], valid)
        for p in range(H_Q // 2):
            heads_of_pair(p, window)

    @pl.when(i == n_blk)
    def _context():
        for p in range(H_Q // 2):
            heads_of_pair(p, None)


def _sw_attn(sinks, wq, wk, wv, seq):
    bsz, ltot, _ = wq.shape
    n_blk = seq // TOK_TILE
    n_steps = ltot // TOK_TILE
    return pl.pallas_call(
        functools.partial(_sw_kernel, n_blk=n_blk, seq=seq),
        grid_spec=pltpu.PrefetchScalarGridSpec(
            num_scalar_prefetch=1,
            grid=(bsz, n_steps),
            in_specs=[
                pl.BlockSpec((1, TOK_TILE, D_SW), lambda b, i, s: (b, i, 0)),
                pl.BlockSpec((1, ltot, D_KV), lambda b, i, s: (b, 0, 0)),
                pl.BlockSpec((1, ltot, D_KV), lambda b, i, s: (b, 0, 0)),
            ],
            out_specs=pl.BlockSpec((1, TOK_TILE, D_SW), lambda b, i, s: (b, i, 0)),
        ),
        out_shape=jax.ShapeDtypeStruct((bsz, ltot, D_SW), BF16),
        compiler_params=pltpu.CompilerParams(
            dimension_semantics=("arbitrary", "arbitrary"), vmem_limit_bytes=VMEM_LIMIT),
        name="sw_attn",
    )(sinks, wq, wk, wv)


def _split_bf16(x):
    hi = x.astype(BF16)
    lo = (x - hi.astype(F32)).astype(BF16)
    return hi, lo


def _merge_kernel(x_ref, mod_ref, g_ref, z_ref, zp_ref, zn_ref, ab_ref, yb_ref, yc_ref, gate_ref,
                  convw_ref, wb_ref, wo_ref, wr_hi_ref, wr_lo_ref, br_ref,
                  xo_ref, h2_ref, rt_ref, cnt_ref, *, n_lat):
    i = pl.program_id(1)
    n_tiles = pl.num_programs(1)
    tm = x_ref.shape[1]

    z = z_ref[0].astype(F32)
    halo = zp_ref.shape[1]
    first = (i == 0) | (i == n_lat)
    last = (i == n_lat - 1) | (i == n_tiles - 1)
    prev_row = jnp.where(first, 0.0, zp_ref[0, halo - 1:halo, :].astype(F32))
    next_row = jnp.where(last, 0.0, zn_ref[0, 0:1, :].astype(F32))
    row = lax.broadcasted_iota(jnp.int32, (tm, 1), 0)
    z_prev = jnp.where(row == 0, prev_row, pltpu.roll(z, 1, axis=0))
    z_next = jnp.where(row == tm - 1, next_row, pltpu.roll(z, tm - 1, axis=0))
    cw = convw_ref[...]
    y_a = ab_ref[0].astype(F32) * (z_prev * cw[0:1] + z * cw[1:2] + z_next * cw[2:3])

    o1 = D_CONV
    o2 = D_CONV + D_NA
    g_a = gate_ref[0, :, 0:D_MODEL].astype(F32)
    g_b = gate_ref[0, :, D_MODEL:2 * D_MODEL].astype(F32)
    g_c = gate_ref[0, :, 2 * D_MODEL:3 * D_MODEL].astype(F32)
    m = (g_a * _dot(y_a.astype(BF16), wb_ref[0:o1, :])
         + g_b * _dot(yb_ref[0], wb_ref[o1:o2, :])
         + g_c * _dot(yc_ref[0], wb_ref[o2:, :]))
    y = _dot(m.astype(BF16), wo_ref[...])
    x_new = x_ref[0] + mod_ref[0, 2:3, :] * y
    xo_ref[0] = x_new

    h2 = _modln(x_new, g_ref[...], mod_ref[0, 3:4, :], mod_ref[0, 4:5, :])
    _store_row_tiles(h2_ref, h2)

    h_hi, h_lo = _split_bf16(h2)
    lg = (_dot(h_hi, wr_hi_ref[...]) + _dot(h_hi, wr_lo_ref[...]) + _dot(h_lo, wr_hi_ref[...])
          + br_ref[...])
    lane = lax.broadcasted_iota(jnp.int32, lg.shape, 1)
    big = jnp.int32(LANES)

    def masked_argmax(vals, mask):
        v = jnp.where(mask, vals, -jnp.inf)
        vmax = v.max(axis=-1, keepdims=True)
        idx = jnp.where(mask & (v == vmax), lane, big).min(axis=-1, keepdims=True)
        return vmax, idx

    is_grp = lane < N_GROUPS
    g_max, g_idx = masked_argmax(lg, is_grp)
    pg_top = 1.0 / jnp.where(is_grp, jnp.exp(lg - g_max), 0.0).sum(axis=-1, keepdims=True)
    e_lo = 8 + g_idx * EXP_PER_GROUP
    in_grp = (lane >= e_lo) & (lane < e_lo + EXP_PER_GROUP)
    v1, i1 = masked_argmax(lg, in_grp)
    v2, i2 = masked_argmax(lg, in_grp & (lane != i1))
    e21 = jnp.exp(v2 - v1)
    w1 = pg_top / (1.0 + e21)
    w2 = pg_top * e21 / (1.0 + e21)

    @pl.when((pl.program_id(0) == 0) & (i == 0))
    def _init_counts():
        cnt_ref[...] = jnp.zeros_like(cnt_ref)

    oh1 = (lane == i1).astype(F32)
    oh2 = (lane == i2).astype(F32)
    ohs = oh1 + oh2
    tri = (lax.broadcasted_iota(jnp.int32, (tm, tm), 0) > lax.broadcasted_iota(jnp.int32, (tm, tm), 1))
    before = _dot(tri.astype(BF16), ohs.astype(BF16)) + cnt_ref[0:1, :]
    pos1 = (before * oh1).sum(axis=-1, keepdims=True)
    pos2 = (before * oh2).sum(axis=-1, keepdims=True)
    cnt_ref[...] = cnt_ref[...] + ohs.sum(axis=0, keepdims=True)

    vals = ((i1 - 8).astype(F32), (i2 - 8).astype(F32), w1, w2, pos1, pos2)
    rt = jnp.zeros_like(lg)
    for j, v in enumerate(vals):
        rt = jnp.where(lane == j, v, rt)
    rt_ref[0] = rt


def _merge(x_all, mod8, g, z, ab, y_b, y_c, gates, conv_w, wb, wo, wr_hi, wr_lo, br, n_lat):
    bsz, ltot, d = x_all.shape
    n_tiles = ltot // TOK_TILE
    tm = TOK_TILE
    halo = 16
    hb = tm // halo
    n_halo = ltot // halo

    def tok_spec(width):
        return pl.BlockSpec((1, tm, width), lambda b, i: (b, i, 0))

    def mod_map(b, i):
        return (jnp.where(i < n_lat, b, bsz), 0, 0)

    def const2(shape):
        return pl.BlockSpec(shape, lambda b, i: (0, 0))

    return pl.pallas_call(
        functools.partial(_merge_kernel, n_lat=n_lat),
        grid=(bsz, n_tiles),
        in_specs=[
            tok_spec(d),
            pl.BlockSpec((1, 8, d), mod_map),
            const2((1, d)),
            tok_spec(D_CONV),
            pl.BlockSpec((1, halo, D_CONV), lambda b, i: (b, jnp.maximum(i * hb - 1, 0), 0)),
            pl.BlockSpec((1, halo, D_CONV), lambda b, i: (b, jnp.minimum((i + 1) * hb, n_halo - 1), 0)),
            tok_spec(D_CONV),
            tok_spec(D_NA),
            tok_spec(D_SW),
            tok_spec(3 * d),
            const2((8, D_CONV)),
            const2((d, d)),
            const2((d, d)),
            const2((d, LANES)),
            const2((d, LANES)),
            const2((1, LANES)),
        ],
        out_specs=[tok_spec(d),
                   pl.BlockSpec((tm * ROW_SUB, LANES), lambda b, i: (b * n_tiles + i, 0)),
                   tok_spec(LANES),
                   const2((8, LANES))],
        out_shape=[jax.ShapeDtypeStruct((bsz, ltot, d), F32),
                   jax.ShapeDtypeStruct((bsz * ltot * ROW_SUB, LANES), F32),
                   jax.ShapeDtypeStruct((bsz, ltot, LANES), F32),
                   jax.ShapeDtypeStruct((8, LANES), F32)],
        compiler_params=pltpu.CompilerParams(
            dimension_semantics=("arbitrary", "arbitrary"), vmem_limit_bytes=VMEM_LIMIT),
        name="merge",
    )(x_all, mod8, g, z, z, z, ab, y_b, y_c, gates, conv_w, wb, wo, wr_hi, wr_lo, br)


def _dispatch_kernel(dest_ref, pad_lo_ref, pad_hi_ref, h2_ref, xd_hbm, zero_buf, sem, zsem,
                     *, pads_per_step):
    s = pl.program_id(0)
    tm = h2_ref.shape[0] // ROW_SUB

    def row_copy(r, k):
        dst = pl.multiple_of(dest_ref[(s * tm + r) * 2 + k], ROW_SUB)
        src = pl.multiple_of(r * ROW_SUB, ROW_SUB)
        return pltpu.make_async_copy(h2_ref.at[pl.ds(src, ROW_SUB)], xd_hbm.at[pl.ds(dst, ROW_SUB)], sem)

    def body(r, carry):
        for k in range(2):
            row_copy(r, k).start(priority=k)
        return carry
    lax.fori_loop(0, tm, body, 0, unroll=DMA_UNROLL)

    def zero_copy(row):
        dst = pl.multiple_of(row * ROW_SUB, ROW_SUB)
        return pltpu.make_async_copy(zero_buf, xd_hbm.at[pl.ds(dst, ROW_SUB)], zsem)

    n_ranges = pad_lo_ref.shape[0]
    for j in range(pads_per_step):
        idx = s * pads_per_step + j

        @pl.when(idx < n_ranges)
        def _fill():
            zero_buf[...] = jnp.zeros_like(zero_buf)
            lo = pad_lo_ref[idx]
            hi = pad_hi_ref[idx]
            lax.fori_loop(lo, hi, lambda row, c: (zero_copy(row).start(), c)[1], 0)
            lax.fori_loop(lo, hi, lambda row, c: (zero_copy(row).wait(), c)[1], 0)

    for k in range(2):
        pltpu.make_async_copy(h2_ref, xd_hbm.at[pl.ds(0, tm * ROW_SUB)], sem).wait()


def _dispatch(dest8, pad_lo, pad_hi, h2t, n_blocks):
    tm = TOK_TILE
    n_steps = h2t.shape[0] // (tm * ROW_SUB)
    xd_rows = n_blocks * MOE_BLK * ROW_SUB
    pads_per_step = -(-pad_lo.shape[0] // n_steps)
    return pl.pallas_call(
        functools.partial(_dispatch_kernel, pads_per_step=pads_per_step),
        grid_spec=pltpu.PrefetchScalarGridSpec(
            num_scalar_prefetch=3,
            grid=(n_steps,),
            in_specs=[pl.BlockSpec((tm * ROW_SUB, LANES), lambda s, d8, plo, phi: (s, 0))],
            out_specs=pl.BlockSpec(memory_space=pl.ANY),
            scratch_shapes=[pltpu.VMEM((ROW_SUB, LANES), F32), pltpu.SemaphoreType.DMA(()),
                            pltpu.SemaphoreType.DMA(())],
        ),
        out_shape=jax.ShapeDtypeStruct((xd_rows, LANES), F32),
        compiler_params=pltpu.CompilerParams(
            dimension_semantics=("arbitrary",), vmem_limit_bytes=VMEM_LIMIT),
        name="moe_dispatch",
    )(dest8, pad_lo, pad_hi, h2t)


def _moe_kernel(blk_e_ref, nused_ref, xd_hbm, wg_ref, wu_ref, wd_ref, yd_ref, wg_s, wu_s, wd_s,
                xbuf, xsem):
    i = pl.program_id(0)
    nused = nused_ref[0]
    blk_rows = MOE_BLK * ROW_SUB

    def block_copy(j):
        src = pl.multiple_of(j * blk_rows, blk_rows)
        return pltpu.make_async_copy(xd_hbm.at[pl.ds(src, blk_rows)], xbuf.at[j % MOE_RING],
                                     xsem.at[j % MOE_RING])

    for j in range(MOE_RING - 1):
        @pl.when((i == 0) & (j < nused))
        def _prime():
            block_copy(j).start()

    @pl.when(i + MOE_RING - 1 < nused)
    def _prefetch():
        block_copy(i + MOE_RING - 1).start()

    @pl.when((i == 0) | (blk_e_ref[i] != blk_e_ref[jnp.maximum(i - 1, 0)]))
    def _cast_weights():
        wg_s[...] = wg_ref[0, 0].astype(BF16)
        wu_s[...] = wu_ref[0, 0].astype(BF16)
        wd_s[...] = wd_ref[0, 0].astype(BF16)

    @pl.when(i < nused)
    def _compute():
        block_copy(i).wait()
        xb = _load_row_tiles(xbuf.at[i % MOE_RING], MOE_BLK).astype(BF16)
        gate = _dot(xb, wg_s[...])
        up = _dot(xb, wu_s[...])
        hid = (gate * _sigmoid(gate)) * up
        _store_row_tiles(yd_ref, _dot(hid.astype(BF16), wd_s[...]))

    @pl.when(i >= nused)
    def _unused():
        yd_ref[...] = jnp.zeros_like(yd_ref)


def _moe(blk_e, nused, xd, wg, wu, wd, layer):
    nb = blk_e.shape[0]
    d = wg.shape[2]
    blk_rows = MOE_BLK * ROW_SUB

    def w_spec(shape):
        return pl.BlockSpec(shape, lambda i, be, nu: (layer, be[i], 0, 0))

    return pl.pallas_call(
        _moe_kernel,
        grid_spec=pltpu.PrefetchScalarGridSpec(
            num_scalar_prefetch=2,
            grid=(nb,),
            in_specs=[
                pl.BlockSpec(memory_space=pl.ANY),
                w_spec((1, 1, d, D_EXPERT)),
                w_spec((1, 1, d, D_EXPERT)),
                w_spec((1, 1, D_EXPERT, d)),
            ],
            out_specs=pl.BlockSpec((blk_rows, LANES), lambda i, be, nu: (i, 0)),
            scratch_shapes=[pltpu.VMEM((d, D_EXPERT), BF16), pltpu.VMEM((d, D_EXPERT), BF16),
                            pltpu.VMEM((D_EXPERT, d), BF16),
                            pltpu.VMEM((MOE_RING, blk_rows, LANES), F32),
                            pltpu.SemaphoreType.DMA((MOE_RING,))],
        ),
        out_shape=jax.ShapeDtypeStruct((nb * blk_rows, LANES), F32),
        compiler_params=pltpu.CompilerParams(
            dimension_semantics=("arbitrary",), vmem_limit_bytes=VMEM_LIMIT),
        name="moe_experts",
    )(blk_e, nused, xd, wg, wu, wd)


def _combine_kernel(dest_ref, x_ref, mod_ref, g_ref, rt_ref, yd_hbm, o_ref, buf, sem, *, final, tile_of):
    s = pl.program_id(0)
    n_steps = pl.num_programs(0)
    tm = x_ref.shape[0]
    slot = s % 2

    def start_gather(step, dst_slot):
        base = tile_of(step) * tm

        def body(r, carry):
            for k in range(2):
                src = pl.multiple_of(dest_ref[(base + r) * 2 + k], ROW_SUB)
                dst = pl.multiple_of(r * ROW_SUB, ROW_SUB)
                pltpu.make_async_copy(yd_hbm.at[pl.ds(src, ROW_SUB)], buf.at[dst_slot, k, pl.ds(dst, ROW_SUB)],
                                      sem.at[dst_slot]).start(priority=k)
            return carry
        lax.fori_loop(0, tm, body, 0, unroll=DMA_UNROLL)

    @pl.when(s == 0)
    def _prime():
        start_gather(0, 0)

    @pl.when(s + 1 < n_steps)
    def _prefetch():
        start_gather(s + 1, 1 - slot)

    for k in range(2):
        pltpu.make_async_copy(yd_hbm.at[pl.ds(0, tm * ROW_SUB)], buf.at[slot, k], sem.at[slot]).wait()
    w0 = rt_ref[:, 2:3]
    w1 = rt_ref[:, 3:4]
    y = w0 * _load_row_tiles(buf.at[slot, 0], tm) + w1 * _load_row_tiles(buf.at[slot, 1], tm)
    x_new = x_ref[...] + mod_ref[0, 5:6, :] * y
    if final:
        ms = jnp.mean(x_new * x_new, axis=-1, keepdims=True)
        x_new = x_new * lax.rsqrt(ms + EPS) * g_ref[...]
    o_ref[...] = x_new


def _combine(dest, x_flat, mod8, g_final, rt, yd, n_lat, n_tiles, final):
    ntok, d = x_flat.shape
    tm = TOK_TILE
    bsz = ntok // (tm * n_tiles)
    n_per = n_lat if final else n_tiles
    n_steps = bsz * n_per

    def tile_of(step):
        return (step // n_per) * n_tiles + step % n_per

    def mod_map(s, dref):
        return (jnp.where(s % n_per < n_lat, s // n_per, bsz), 0, 0)

    return pl.pallas_call(
        functools.partial(_combine_kernel, final=final, tile_of=tile_of),
        grid_spec=pltpu.PrefetchScalarGridSpec(
            num_scalar_prefetch=1,
            grid=(n_steps,),
            in_specs=[
                pl.BlockSpec((tm, d), lambda s, dref: (tile_of(s), 0)),
                pl.BlockSpec((1, 8, d), mod_map),
                pl.BlockSpec((1, d), lambda s, dref: (0, 0)),
                pl.BlockSpec((tm, LANES), lambda s, dref: (tile_of(s), 0)),
                pl.BlockSpec(memory_space=pl.ANY),
            ],
            out_specs=pl.BlockSpec((tm, d), lambda s, dref: (s, 0)),
            scratch_shapes=[pltpu.VMEM((2, 2, tm * ROW_SUB, LANES), F32), pltpu.SemaphoreType.DMA((2,))],
        ),
        out_shape=jax.ShapeDtypeStruct((n_steps * tm, d), F32),
        compiler_params=pltpu.CompilerParams(
            dimension_semantics=("arbitrary",), vmem_limit_bytes=VMEM_LIMIT),
        name="combine",
    )(dest, x_flat, mod8, g_final, rt, yd)


def _rope_tables(seq, ctx_len):
    quarter = HEAD_DIM // 4
    rows = seq // GRID_W
    inv = ROPE_BASE ** (-jnp.arange(quarter, dtype=F32) / quarter)
    ang_r = jnp.arange(rows, dtype=F32)[:, None] * inv[None, :]
    ang_c = jnp.arange(GRID_W, dtype=F32)[:, None] * inv[None, :]

    def by_row(a):
        return jnp.broadcast_to(a[:, None, :], (rows, GRID_W, quarter)).reshape(seq, quarter)

    def by_col(a):
        return jnp.broadcast_to(a[None, :, :], (rows, GRID_W, quarter)).reshape(seq, quarter)

    cos_r, sin_r = by_row(jnp.cos(ang_r)), by_row(jnp.sin(ang_r))
    cos_c, sin_c = by_col(jnp.cos(ang_c)), by_col(jnp.sin(ang_c))
    cos = jnp.concatenate([cos_r, cos_r, cos_c, cos_c], axis=-1)
    sin = jnp.concatenate([-sin_r, sin_r, -sin_c, sin_c], axis=-1)
    cos = jnp.concatenate([cos, jnp.ones((ctx_len, HEAD_DIM), F32)], axis=0)
    sin = jnp.concatenate([sin, jnp.zeros((ctx_len, HEAD_DIM), F32)], axis=0)
    return jnp.concatenate([cos, cos], axis=-1), jnp.concatenate([sin, sin], axis=-1)


def _na_bias_tables(rpb, rows):
    qc = np.arange(GRID_W)
    kc = np.arange(GRID_W)
    cstart = np.clip(qc - WIN_C // 2, 0, GRID_W - WIN_C)
    rel_c = np.clip(kc[None, :] - qc[:, None] + WIN_C - 1, 0, 2 * WIN_C - 2)
    col_ok = (kc[None, :] >= cstart[:, None]) & (kc[None, :] < cstart[:, None] + WIN_C)
    t = jnp.where(col_ok[None, None], rpb[:, :, rel_c], NEG_INF)
    masked = jnp.full((rpb.shape[0], GRID_W, GRID_W), NEG_INF, F32)
    n_blk = rows // NA_ROWS
    cases = []
    for blk in (0, min(1, n_blk - 1), n_blk - 1):
        r0 = blk * NA_ROWS
        ks = int(np.clip(r0 - WIN_R // 2, 0, rows - NA_KROWS))
        q_rows = []
        for qr in range(NA_ROWS):
            r = r0 + qr
            rs = int(np.clip(r - WIN_R // 2, 0, rows - WIN_R))
            tiles = []
            for a in range(NA_KROWS):
                kr = ks + a
                tiles.append(t[:, kr - r + WIN_R - 1] if rs <= kr < rs + WIN_R else masked)
            q_rows.append(jnp.concatenate(tiles, axis=-1))
        cases.append(jnp.concatenate(q_rows, axis=-2))
    return jnp.stack(cases, axis=0).astype(F32)


def _routing_plan(rt, cnt, n_blocks):
    counts = cnt[0, 8:8 + N_EXPERTS].astype(jnp.int32)
    padded = ((counts + MOE_BLK - 1) // MOE_BLK) * MOE_BLK
    pad_end = jnp.cumsum(padded)
    pad_start = pad_end - padded
    eid = rt[:, 0:2].astype(jnp.int32)
    pos = rt[:, 4:6].astype(jnp.int32)
    start_of = jnp.sum(jnp.where(eid[:, :, None] == jnp.arange(N_EXPERTS)[None, None, :],
                                 pad_start[None, None, :], 0), axis=-1)
    dest8 = ((start_of + pos) * ROW_SUB).reshape(-1)
    blk_start = jnp.arange(n_blocks, dtype=jnp.int32) * MOE_BLK
    blk_e = jnp.clip(jnp.sum(pad_end[None, :] <= blk_start[:, None], axis=1), 0, N_EXPERTS - 1).astype(jnp.int32)
    nused = (pad_end[-1] // MOE_BLK).astype(jnp.int32).reshape(1)
    total = jnp.full((1,), n_blocks * MOE_BLK, jnp.int32)
    pad_lo = jnp.concatenate([pad_start + counts, pad_end[-1:]]).astype(jnp.int32)
    pad_hi = jnp.concatenate([pad_end, total]).astype(jnp.int32)
    return blk_e, nused, dest8, pad_lo, pad_hi


def kernel(x, c, ctx, c_ctx, w_mod, b_mod, g_attn, g_ffn, w_in, conv_w, rpb, sinks, w_branch, w_o,
           w_rg, b_rg, w_re, b_re, w_e_gate, w_e_up, w_e_down, g_final):
    bsz, seq, d = x.shape
    ctx_len = ctx.shape[1]
    depth = w_in.shape[0]
    ltot = seq + ctx_len
    n_lat = seq // TOK_TILE
    n_tiles = ltot // TOK_TILE
    rows = seq // GRID_W
    ntok = bsz * ltot
    n_assign = ntok * 2
    n_blocks = (n_assign + N_EXPERTS * (MOE_BLK - 1) + MOE_BLK - 1) // MOE_BLK

    x_all = jnp.concatenate([x, ctx], axis=1)
    cos_t, sin_t = _rope_tables(seq, ctx_len)
    assert bsz + 1 <= 8
    c8 = jnp.pad(jnp.concatenate([c, c_ctx[None, :]], axis=0), ((0, 7 - bsz), (0, 0)))
    mod_all = _mod_vectors(c8, w_mod, b_mod)
    w_in_bf = w_in.astype(BF16)
    w_branch_bf = w_branch.astype(BF16)
    w_o_bf = w_o.astype(BF16)

    for l in range(depth):
        mod8 = jnp.pad(mod_all[l, :bsz + 1].reshape(bsz + 1, 6, d), ((0, 0), (0, 2), (0, 0)))
        w_l = w_in_bf[l]
        bias = _na_bias_tables(rpb[l], rows)
        o2 = D_CONV + D_NA
        wb = jnp.concatenate([w_branch_bf[l, :o2]] + [w_branch_bf[l, o2 + h * HEAD_DIM:o2 + (h + 1) * HEAD_DIM]
                                                      for h in _SW_HEAD_ORDER], axis=0)
        pad_g = jnp.zeros((d, 8 - N_GROUPS), F32)
        pad_e = jnp.zeros((d, LANES - 8 - N_EXPERTS), F32)
        wr = jnp.concatenate([w_rg[l], pad_g, w_re[l], pad_e], axis=1)
        wr_hi = wr.astype(BF16)
        wr_lo = (wr - wr_hi.astype(F32)).astype(BF16)
        br = jnp.concatenate([b_rg[l], pad_g[0], b_re[l], pad_e[0]])[None, :]
        convw8 = jnp.pad(conv_w[l], ((0, 8 - CONV_W), (0, 0)))

        z, ab, nq, nk, nv, wq, wk, wv, gates = _inproj(
            x_all, mod8, g_attn[l][None, :], w_l, cos_t, sin_t, n_lat)
        y_b = _na_attn(nq, nk, nv, bias, seq)
        y_c = _sw_attn(sinks[l], wq, wk, wv, seq)
        x_mid, h2t, rt, cnt = _merge(x_all, mod8, g_ffn[l][None, :], z, ab, y_b, y_c, gates, convw8, wb,
                                     w_o_bf[l], wr_hi, wr_lo, br, n_lat)
        rt = rt.reshape(ntok, LANES)
        blk_e, nused, dest8, pad_lo, pad_hi = _routing_plan(rt, cnt, n_blocks)
        xd = _dispatch(dest8, pad_lo, pad_hi, h2t, n_blocks)
        yd = _moe(blk_e, nused, xd, w_e_gate, w_e_up, w_e_down, l)
        final = l == depth - 1
        x_flat = _combine(dest8, x_mid.reshape(ntok, d), mod8, g_final[None, :], rt, yd, n_lat, n_tiles,
                          final)
        if not final:
            x_all = x_flat.reshape(bsz, ltot, d)
    return x_flat.reshape(bsz, seq, d)
```

```python
import functools

import numpy as np
import jax
import jax.numpy as jnp
from jax import lax
from jax.experimental import pallas as pl
from jax.experimental.pallas import tpu as pltpu

F32 = jnp.float32
BF16 = jnp.bfloat16

D_MODEL = 1024
GRID_W = 64
HEAD_DIM = 64
D_CONV = 256
CONV_W = 3
H_NA = 6
WIN_R = 8
WIN_C = 16
H_Q = 6
H_KV = 2
WINDOW = 128
ROPE_BASE = 10000.0
D_NA = H_NA * HEAD_DIM
D_SW = H_Q * HEAD_DIM
D_KV = H_KV * HEAD_DIM
N_GROUPS = 4
EXP_PER_GROUP = 8
N_EXPERTS = N_GROUPS * EXP_PER_GROUP
D_EXPERT = 256
MOE_BLK = 256
EPS = 1e-6
NEG_INF = -1e30

LANES = 128
ROW_SUB = D_MODEL // LANES
TOK_TILE = 256
INPROJ_SUB = 2
NA_ROWS = TOK_TILE // GRID_W
NA_KROWS = NA_ROWS + WIN_R - 1
NA_KEYS = NA_KROWS * GRID_W
SW_KEYS = TOK_TILE + 2 * WINDOW
VMEM_LIMIT = 56 * 1024 * 1024
DMA_UNROLL = 8
MOE_RING = 3
DISPATCH_RING = 3

_C_CONV = 0
_C_NQ = _C_CONV + 3 * D_CONV
_C_NK = _C_NQ + D_NA
_C_NV = _C_NK + D_NA
_C_WQ = _C_NV + D_NA
_C_WK = _C_WQ + D_SW
_C_WV = _C_WK + D_KV
_C_GATE = _C_WV + D_KV
_C_END = _C_GATE + 3 * D_MODEL

_SW_HEAD_ORDER = (0, 3, 1, 4, 2, 5)


def _dot(a, b):
    return jnp.dot(a, b, preferred_element_type=F32)


def _dot_nt(a, b):
    return lax.dot_general(a, b, (((1,), (1,)), ((), ())), preferred_element_type=F32)


def _sigmoid(x):
    return 1.0 / (1.0 + jnp.exp(-x))


def _modln(x, g, shift, scale):
    ms = jnp.mean(x * x, axis=-1, keepdims=True)
    return (x * lax.rsqrt(ms + EPS)) * g * (1.0 + scale) + shift


def _store_row_tiles(ref, val):
    n = val.shape[0]
    for s in range(ROW_SUB):
        ref[pl.ds(s, n, stride=ROW_SUB), :] = val[:, s * LANES:(s + 1) * LANES]


def _load_row_tiles(ref, n):
    return jnp.concatenate([ref[pl.ds(s, n, stride=ROW_SUB), :] for s in range(ROW_SUB)], axis=-1)


def _mod_kernel(c_ref, w_ref, b_ref, o_ref):
    cv = c_ref[...]
    sc = (cv * _sigmoid(cv)).astype(BF16)
    o_ref[0] = _dot(sc, w_ref[0].astype(BF16)) + b_ref[0]


def _mod_vectors(c8, w_mod, b_mod):
    depth, d, cols = w_mod.shape
    tn = cols // 4
    return pl.pallas_call(
        _mod_kernel,
        grid=(depth, cols // tn),
        in_specs=[
            pl.BlockSpec((8, d), lambda l, j: (0, 0)),
            pl.BlockSpec((1, d, tn), lambda l, j: (l, 0, j)),
            pl.BlockSpec((1, 1, tn), lambda l, j: (l, 0, j)),
        ],
        out_specs=pl.BlockSpec((1, 8, tn), lambda l, j: (l, 0, j)),
        out_shape=jax.ShapeDtypeStruct((depth, 8, cols), F32),
        compiler_params=pltpu.CompilerParams(
            dimension_semantics=("arbitrary", "arbitrary"), vmem_limit_bytes=VMEM_LIMIT),
        name="mod_vectors",
    )(c8, w_mod, b_mod.reshape(depth, 1, cols))


def _inproj_kernel(*refs):
    ns = INPROJ_SUB
    x_ref, g_ref, w_ref = refs[0], refs[1], refs[2]
    mod_refs = refs[3:3 + ns]
    cos_refs = refs[3 + ns:3 + 2 * ns]
    sin_refs = refs[3 + 2 * ns:3 + 3 * ns]
    z_ref, ab_ref, nq_ref, nk_ref, nv_ref, wq_ref, wk_ref, wv_ref, gate_ref = refs[3 + 3 * ns:]

    h = jnp.concatenate(
        [_modln(x_ref[j * TOK_TILE:(j + 1) * TOK_TILE, :], g_ref[...],
                mod_refs[j][0, 0:1, :], mod_refs[j][0, 1:2, :]).astype(BF16) for j in range(ns)], axis=0)

    def proj(c0, c1):
        return _dot(h, w_ref[:, c0:c1])

    a_h = proj(_C_CONV, _C_CONV + D_CONV)
    a_b = proj(_C_CONV + D_CONV, _C_CONV + 2 * D_CONV)
    a_c = proj(_C_CONV + 2 * D_CONV, _C_NQ)
    z_ref[...] = (a_c * a_h).astype(BF16)
    ab_ref[...] = a_b.astype(BF16)

    qk_scale = HEAD_DIM ** -0.5
    nq_ref[...] = (proj(_C_NQ, _C_NK) * qk_scale).astype(BF16)
    nk_ref[...] = proj(_C_NK, _C_NV).astype(BF16)
    nv_ref[...] = proj(_C_NV, _C_WQ).astype(BF16)

    cos = jnp.concatenate([r[...] for r in cos_refs], axis=0)
    sin = jnp.concatenate([r[...] for r in sin_refs], axis=0)
    cos3 = jnp.concatenate([cos] * (D_SW // LANES), axis=-1)
    sin3 = jnp.concatenate([sin] * (D_SW // LANES), axis=-1)

    def rope(v, cos_v, sin_v):
        quarter = HEAD_DIM // 4
        width = v.shape[-1]
        lane = lax.broadcasted_iota(jnp.int32, (1, width), 1)
        first = (lane % (2 * quarter)) < quarter
        swapped = jnp.where(first, pltpu.roll(v, width - quarter, axis=1), pltpu.roll(v, quarter, axis=1))
        return v * cos_v + swapped * sin_v

    wq = rope(proj(_C_WQ, _C_WK), cos3, sin3) * qk_scale
    wq = jnp.concatenate([wq[:, h * HEAD_DIM:(h + 1) * HEAD_DIM] for h in _SW_HEAD_ORDER], axis=1)
    wq_ref[...] = wq.astype(BF16)
    wk_ref[...] = rope(proj(_C_WK, _C_WV), cos, sin).astype(BF16)
    wv_ref[...] = proj(_C_WV, _C_GATE).astype(BF16)

    for j in range(3):
        c0 = _C_GATE + j * D_MODEL
        gate_ref[:, j * D_MODEL:(j + 1) * D_MODEL] = _sigmoid(proj(c0, c0 + D_MODEL)).astype(BF16)


def _inproj(x_all, mod8, g, w, cos_t, sin_t, n_lat):
    bsz, ltot, d = x_all.shape
    n_tiles = ltot // TOK_TILE
    ns = INPROJ_SUB
    tm = ns * TOK_TILE
    ntok = bsz * ltot
    assert ntok % tm == 0

    def tok_spec(width):
        return pl.BlockSpec((tm, width), lambda t: (t, 0))

    def mod_spec(j):
        def index(t):
            u = t * ns + j
            return (jnp.where(u % n_tiles < n_lat, u // n_tiles, bsz), 0, 0)
        return pl.BlockSpec((1, 8, d), index)

    def rope_spec(j):
        return pl.BlockSpec((TOK_TILE, LANES), lambda t: ((t * ns + j) % n_tiles, 0))

    widths = (D_CONV, D_CONV, D_NA, D_NA, D_NA, D_SW, D_KV, D_KV, 3 * D_MODEL)
    outs = pl.pallas_call(
        _inproj_kernel,
        grid=(ntok // tm,),
        in_specs=([tok_spec(d),
                   pl.BlockSpec((1, d), lambda t: (0, 0)),
                   pl.BlockSpec((d, _C_END), lambda t: (0, 0), pipeline_mode=pl.Buffered(1))]
                  + [mod_spec(j) for j in range(ns)]
                  + [rope_spec(j) for j in range(ns)]
                  + [rope_spec(j) for j in range(ns)]),
        out_specs=[tok_spec(wd) for wd in widths],
        out_shape=[jax.ShapeDtypeStruct((ntok, wd), BF16) for wd in widths],
        compiler_params=pltpu.CompilerParams(
            dimension_semantics=("arbitrary",), vmem_limit_bytes=VMEM_LIMIT),
        name="inproj",
    )(x_all.reshape(ntok, d), g, w, *([mod8] * ns), *([cos_t] * ns), *([sin_t] * ns))
    return [o.reshape(bsz, ltot, o.shape[-1]) for o in outs]


def _half_mask(half):
    lane = lax.broadcasted_iota(jnp.int32, (1, LANES), 1)
    return (lane < HEAD_DIM) if half == 0 else (lane >= HEAD_DIM)


def _softmax_pv(parts, sink=None):
    m = parts[0][0].max(axis=-1, keepdims=True)
    for s, _ in parts[1:]:
        m = jnp.maximum(m, s.max(axis=-1, keepdims=True))
    if sink is not None:
        m = jnp.maximum(m, sink)
    l = None
    o = None
    for s, v in parts:
        e = jnp.exp(s - m)
        ls = e.sum(axis=-1, keepdims=True)
        os_ = _dot(e.astype(BF16), v)
        l = ls if l is None else l + ls
        o = os_ if o is None else o + os_
    if sink is not None:
        l = l + jnp.exp(sink - m)
    return o * (1.0 / l)


def _na_kernel(q_ref, k_ref, v_ref, bias_ref, o_ref, *, n_blk, rows, seq):
    i = pl.program_id(1)
    ctx_len = k_ref.shape[1] - seq

    def pair_slices(p):
        return slice(p * LANES, (p + 1) * LANES)

    @pl.when(i < n_blk)
    def _latent():
        r0 = i * NA_ROWS
        ks = jnp.clip(r0 - WIN_R // 2, 0, rows - NA_KROWS)
        kstart = pl.multiple_of(ks * GRID_W, GRID_W)
        for p in range(H_NA // 2):
            ps = pair_slices(p)
            q2 = q_ref[0, :, ps]
            k2 = k_ref[0, pl.ds(kstart, NA_KEYS), ps]
            v2 = v_ref[0, pl.ds(kstart, NA_KEYS), ps]
            kc2 = k_ref[0, seq:seq + ctx_len, ps]
            vc2 = v_ref[0, seq:seq + ctx_len, ps]
            qm = jnp.concatenate([jnp.where(_half_mask(half), q2, jnp.zeros_like(q2)) for half in range(2)],
                                 axis=0)
            bias2 = jnp.concatenate([bias_ref[0, 2 * p], bias_ref[0, 2 * p + 1]], axis=0)
            o2 = _softmax_pv([(_dot_nt(qm, k2) + bias2, v2), (_dot_nt(qm, kc2), vc2)])
            o_ref[0, :, ps] = jnp.where(_half_mask(0), o2[:TOK_TILE], o2[TOK_TILE:]).astype(BF16)

    @pl.when(i == n_blk)
    def _context():
        for p in range(H_NA // 2):
            ps = pair_slices(p)
            q2 = q_ref[0, :, ps]
            kc2 = k_ref[0, seq:seq + ctx_len, ps]
            vc2 = v_ref[0, seq:seq + ctx_len, ps]
            outs = []
            for half in range(2):
                qm = jnp.where(_half_mask(half), q2, jnp.zeros_like(q2))
                outs.append(_softmax_pv([(_dot_nt(qm, kc2), vc2)]))
            o_ref[0, :, ps] = jnp.where(_half_mask(0), outs[0], outs[1]).astype(BF16)


def _na_attn(nq, nk, nv, bias, seq):
    bsz, ltot, _ = nq.shape
    rows = seq // GRID_W
    n_blk = seq // TOK_TILE
    n_steps = ltot // TOK_TILE

    def bias_map(b, i):
        return (jnp.where(i == 0, 0, jnp.where(i >= n_blk - 1, 2, 1)), 0, 0, 0)

    return pl.pallas_call(
        functools.partial(_na_kernel, n_blk=n_blk, rows=rows, seq=seq),
        grid=(bsz, n_steps),
        in_specs=[
            pl.BlockSpec((1, TOK_TILE, D_NA), lambda b, i: (b, i, 0)),
            pl.BlockSpec((1, ltot, D_NA), lambda b, i: (b, 0, 0)),
            pl.BlockSpec((1, ltot, D_NA), lambda b, i: (b, 0, 0)),
            pl.BlockSpec((1, H_NA, TOK_TILE, NA_KEYS), bias_map),
        ],
        out_specs=pl.BlockSpec((1, TOK_TILE, D_NA), lambda b, i: (b, i, 0)),
        out_shape=jax.ShapeDtypeStruct((bsz, ltot, D_NA), BF16),
        compiler_params=pltpu.CompilerParams(
            dimension_semantics=("arbitrary", "arbitrary"), vmem_limit_bytes=VMEM_LIMIT),
        name="na_attn",
    )(nq, nk, nv, bias)


def _sw_kernel(sink_ref, q_ref, k_ref, v_ref, o_ref, *, n_blk, seq):
    i = pl.program_id(1)
    ctx_len = k_ref.shape[1] - seq
    kc2 = k_ref[0, seq:seq + ctx_len, :]
    vc2 = v_ref[0, seq:seq + ctx_len, :]

    row2 = lax.broadcasted_iota(jnp.int32, (2 * TOK_TILE, 1), 0)

    def heads_of_pair(p, window):
        q2 = q_ref[0, :, p * LANES:(p + 1) * LANES]
        qm = jnp.concatenate([jnp.where(_half_mask(half), q2, jnp.zeros_like(q2)) for half in range(2)], axis=0)
        sink = jnp.where(row2 < TOK_TILE, sink_ref[p], sink_ref[H_Q // H_KV + p])
        parts = []
        if window is not None:
            k2, v2, valid = window
            parts.append((jnp.where(valid, _dot_nt(qm, k2), NEG_INF), v2))
        parts.append((_dot_nt(qm, kc2), vc2))
        o2 = _softmax_pv(parts, sink=sink)
        o_ref[0, :, p * LANES:(p + 1) * LANES] = jnp.where(_half_mask(0), o2[:TOK_TILE], o2[TOK_TILE:]).astype(BF16)

    @pl.when(i < n_blk)
    def _latent():
        t0 = i * TOK_TILE
        kst = pl.multiple_of(jnp.clip(t0 - WINDOW, 0, seq - SW_KEYS), WINDOW)
        qpos = t0 + row2 % TOK_TILE
        kpos = kst + lax.broadcasted_iota(jnp.int32, (1, SW_KEYS), 1)
        valid = jnp.abs(qpos - kpos) <= WINDOW
        window = (k_ref[0, pl.ds(kst, SW_KEYS), :], v_ref[0, pl.ds(kst, SW_KEYS), :], valid)
        for p in range(H_Q // 2):
            heads_of_pair(p, window)

    @pl.when(i == n_blk)
    def _context():
        for p in range(H_Q // 2):
            heads_of_pair(p, None)


def _sw_attn(sinks, wq, wk, wv, seq):
    bsz, ltot, _ = wq.shape
    n_blk = seq // TOK_TILE
    n_steps = ltot // TOK_TILE
    return pl.pallas_call(
        functools.partial(_sw_kernel, n_blk=n_blk, seq=seq),
        grid_spec=pltpu.PrefetchScalarGridSpec(
            num_scalar_prefetch=1,
            grid=(bsz, n_steps),
            in_specs=[
                pl.BlockSpec((1, TOK_TILE, D_SW), lambda b, i, s: (b, i, 0)),
                pl.BlockSpec((1, ltot, D_KV), lambda b, i, s: (b, 0, 0)),
                pl.BlockSpec((1, ltot, D_KV), lambda b, i, s: (b, 0, 0)),
            ],
            out_specs=pl.BlockSpec((1, TOK_TILE, D_SW), lambda b, i, s: (b, i, 0)),
        ),
        out_shape=jax.ShapeDtypeStruct((bsz, ltot, D_SW), BF16),
        compiler_params=pltpu.CompilerParams(
            dimension_semantics=("arbitrary", "arbitrary"), vmem_limit_bytes=VMEM_LIMIT),
        name="sw_attn",
    )(sinks, wq, wk, wv)


def _split_bf16(x):
    hi = x.astype(BF16)
    lo = (x - hi.astype(F32)).astype(BF16)
    return hi, lo


def _merge_kernel(x_ref, mod_ref, g_ref, z_ref, zp_ref, zn_ref, ab_ref, yb_ref, yc_ref, gate_ref,
                  convw_ref, wb_ref, wo_ref, wr_hi_ref, wr_lo_ref, br_ref,
                  xo_ref, h2_ref, rt_ref, cnt_ref, *, n_lat):
    i = pl.program_id(1)
    n_tiles = pl.num_programs(1)
    tm = x_ref.shape[1]

    z = z_ref[0].astype(F32)
    halo = zp_ref.shape[1]
    first = (i == 0) | (i == n_lat)
    last = (i == n_lat - 1) | (i == n_tiles - 1)
    prev_row = jnp.where(first, 0.0, zp_ref[0, halo - 1:halo, :].astype(F32))
    next_row = jnp.where(last, 0.0, zn_ref[0, 0:1, :].astype(F32))
    row = lax.broadcasted_iota(jnp.int32, (tm, 1), 0)
    z_prev = jnp.where(row == 0, prev_row, pltpu.roll(z, 1, axis=0))
    z_next = jnp.where(row == tm - 1, next_row, pltpu.roll(z, tm - 1, axis=0))
    cw = convw_ref[...]
    y_a = ab_ref[0].astype(F32) * (z_prev * cw[0:1] + z * cw[1:2] + z_next * cw[2:3])

    o1 = D_CONV
    o2 = D_CONV + D_NA
    g_a = gate_ref[0, :, 0:D_MODEL].astype(F32)
    g_b = gate_ref[0, :, D_MODEL:2 * D_MODEL].astype(F32)
    g_c = gate_ref[0, :, 2 * D_MODEL:3 * D_MODEL].astype(F32)
    m = (g_a * _dot(y_a.astype(BF16), wb_ref[0:o1, :])
         + g_b * _dot(yb_ref[0], wb_ref[o1:o2, :])
         + g_c * _dot(yc_ref[0], wb_ref[o2:, :]))
    y = _dot(m.astype(BF16), wo_ref[...])
    x_new = x_ref[0] + mod_ref[0, 2:3, :] * y
    xo_ref[0] = x_new

    h2 = _modln(x_new, g_ref[...], mod_ref[0, 3:4, :], mod_ref[0, 4:5, :])
    _store_row_tiles(h2_ref, h2)

    h_hi, h_lo = _split_bf16(h2)
    lg = (_dot(h_hi, wr_hi_ref[...]) + _dot(h_hi, wr_lo_ref[...]) + _dot(h_lo, wr_hi_ref[...])
          + br_ref[...])
    lane = lax.broadcasted_iota(jnp.int32, lg.shape, 1)
    lane_f = lane.astype(F32)
    big = float(LANES)

    def masked_argmax(v):
        vmax = v.max(axis=-1, keepdims=True)
        idx = jnp.where(v == vmax, lane_f, big).min(axis=-1, keepdims=True)
        return vmax, idx

    is_grp = lane < N_GROUPS
    g_max, g_idx = masked_argmax(jnp.where(is_grp, lg, -jnp.inf))
    pg_top = 1.0 / jnp.where(is_grp, jnp.exp(lg - g_max), 0.0).sum(axis=-1, keepdims=True)
    e_lo = 8.0 + g_idx * EXP_PER_GROUP
    cand = jnp.where(lane_f >= e_lo, jnp.where(lane_f < e_lo + EXP_PER_GROUP, lg, -jnp.inf), -jnp.inf)
    v1, i1 = masked_argmax(cand)
    v2, i2 = masked_argmax(jnp.where(lane_f == i1, -jnp.inf, cand))
    e21 = jnp.exp(v2 - v1)
    w1 = pg_top / (1.0 + e21)
    w2 = pg_top * e21 / (1.0 + e21)

    @pl.when((pl.program_id(0) == 0) & (i == 0))
    def _init_counts():
        cnt_ref[...] = jnp.zeros_like(cnt_ref)

    oh1 = jnp.where(lane_f == i1, 1.0, 0.0)
    oh2 = jnp.where(lane_f == i2, 1.0, 0.0)
    ohs = oh1 + oh2
    tri = (lax.broadcasted_iota(jnp.int32, (tm, tm), 0) > lax.broadcasted_iota(jnp.int32, (tm, tm), 1))
    before = _dot(tri.astype(BF16), ohs.astype(BF16)) + cnt_ref[0:1, :]
    pos1 = (before * oh1).sum(axis=-1, keepdims=True)
    pos2 = (before * oh2).sum(axis=-1, keepdims=True)
    cnt_ref[...] = cnt_ref[...] + ohs.sum(axis=0, keepdims=True)

    vals = (i1 - 8.0, i2 - 8.0, w1, w2, pos1, pos2)
    rt = jnp.zeros_like(lg)
    for j, v in enumerate(vals):
        rt = jnp.where(lane == j, v, rt)
    rt_ref[0] = rt


def _merge(x_all, mod8, g, z, ab, y_b, y_c, gates, conv_w, wb, wo, wr_hi, wr_lo, br, n_lat):
    bsz, ltot, d = x_all.shape
    n_tiles = ltot // TOK_TILE
    tm = TOK_TILE
    halo = 16
    hb = tm // halo
    n_halo = ltot // halo

    def tok_spec(width):
        return pl.BlockSpec((1, tm, width), lambda b, i: (b, i, 0))

    def mod_map(b, i):
        return (jnp.where(i < n_lat, b, bsz), 0, 0)

    def const2(shape):
        return pl.BlockSpec(shape, lambda b, i: (0, 0))

    return pl.pallas_call(
        functools.partial(_merge_kernel, n_lat=n_lat),
        grid=(bsz, n_tiles),
        in_specs=[
            tok_spec(d),
            pl.BlockSpec((1, 8, d), mod_map),
            const2((1, d)),
            tok_spec(D_CONV),
            pl.BlockSpec((1, halo, D_CONV), lambda b, i: (b, jnp.maximum(i * hb - 1, 0), 0)),
            pl.BlockSpec((1, halo, D_CONV), lambda b, i: (b, jnp.minimum((i + 1) * hb, n_halo - 1), 0)),
            tok_spec(D_CONV),
            tok_spec(D_NA),
            tok_spec(D_SW),
            tok_spec(3 * d),
            const2((8, D_CONV)),
            const2((d, d)),
            const2((d, d)),
            const2((d, LANES)),
            const2((d, LANES)),
            const2((1, LANES)),
        ],
        out_specs=[tok_spec(d),
                   pl.BlockSpec((tm * ROW_SUB, LANES), lambda b, i: (b * n_tiles + i, 0)),
                   tok_spec(LANES),
                   const2((8, LANES))],
        out_shape=[jax.ShapeDtypeStruct((bsz, ltot, d), F32),
                   jax.ShapeDtypeStruct((bsz * ltot * ROW_SUB, LANES), F32),
                   jax.ShapeDtypeStruct((bsz, ltot, LANES), F32),
                   jax.ShapeDtypeStruct((8, LANES), F32)],
        compiler_params=pltpu.CompilerParams(
            dimension_semantics=("arbitrary", "arbitrary"), vmem_limit_bytes=VMEM_LIMIT),
        name="merge",
    )(x_all, mod8, g, z, z, z, ab, y_b, y_c, gates, conv_w, wb, wo, wr_hi, wr_lo, br)


def _dispatch_kernel(dest_ref, pad_lo_ref, pad_hi_ref, h2_hbm, xd_hbm, hbuf, zero_buf, in_sem, out_sem, zsem,
                     *, pads_per_step):
    s = pl.program_id(0)
    n_steps = pl.num_programs(0)
    tm = hbuf.shape[1] // ROW_SUB
    tile_rows = tm * ROW_SUB

    def tile_copy(t):
        src = pl.multiple_of(t * tile_rows, tile_rows)
        return pltpu.make_async_copy(h2_hbm.at[pl.ds(src, tile_rows)], hbuf.at[t % DISPATCH_RING],
                                     in_sem.at[t % DISPATCH_RING])

    def drain_scatters(t):
        for k in range(2):
            pltpu.make_async_copy(hbuf.at[t % DISPATCH_RING], xd_hbm.at[pl.ds(0, tile_rows)],
                                  out_sem.at[t % DISPATCH_RING]).wait()

    for t in range(DISPATCH_RING - 1):
        @pl.when((s == 0) & (t < n_steps))
        def _prime():
            tile_copy(t).start()

    tile_copy(s).wait()
    slot = s % DISPATCH_RING

    def body(r, carry):
        for k in range(2):
            dst = pl.multiple_of(dest_ref[(s * tm + r) * 2 + k], ROW_SUB)
            src = pl.multiple_of(r * ROW_SUB, ROW_SUB)
            pltpu.make_async_copy(hbuf.at[slot, pl.ds(src, ROW_SUB)], xd_hbm.at[pl.ds(dst, ROW_SUB)],
                                  out_sem.at[slot]).start(priority=k)
        return carry
    lax.fori_loop(0, tm, body, 0, unroll=DMA_UNROLL)

    @pl.when(s >= 1)
    def _drain_previous():
        drain_scatters(s - 1)

    @pl.when(s + DISPATCH_RING - 1 < n_steps)
    def _prefetch():
        tile_copy(s + DISPATCH_RING - 1).start()

    def zero_copy(row):
        dst = pl.multiple_of(row * ROW_SUB, ROW_SUB)
        return pltpu.make_async_copy(zero_buf, xd_hbm.at[pl.ds(dst, ROW_SUB)], zsem)

    n_ranges = pad_lo_ref.shape[0]
    for j in range(pads_per_step):
        idx = s * pads_per_step + j

        @pl.when(idx < n_ranges)
        def _fill():
            zero_buf[...] = jnp.zeros_like(zero_buf)
            lo = pad_lo_ref[idx]
            hi = pad_hi_ref[idx]
            lax.fori_loop(lo, hi, lambda row, c: (zero_copy(row).start(), c)[1], 0)
            lax.fori_loop(lo, hi, lambda row, c: (zero_copy(row).wait(), c)[1], 0)

    @pl.when(s == n_steps - 1)
    def _drain_last():
        drain_scatters(s)


def _dispatch(dest8, pad_lo, pad_hi, h2t, n_blocks):
    tm = TOK_TILE
    n_steps = h2t.shape[0] // (tm * ROW_SUB)
    xd_rows = n_blocks * MOE_BLK * ROW_SUB
    pads_per_step = -(-pad_lo.shape[0] // n_steps)
    return pl.pallas_call(
        functools.partial(_dispatch_kernel, pads_per_step=pads_per_step),
        grid_spec=pltpu.PrefetchScalarGridSpec(
            num_scalar_prefetch=3,
            grid=(n_steps,),
            in_specs=[pl.BlockSpec(memory_space=pl.ANY)],
            out_specs=pl.BlockSpec(memory_space=pl.ANY),
            scratch_shapes=[pltpu.VMEM((DISPATCH_RING, tm * ROW_SUB, LANES), F32),
                            pltpu.VMEM((ROW_SUB, LANES), F32),
                            pltpu.SemaphoreType.DMA((DISPATCH_RING,)),
                            pltpu.SemaphoreType.DMA((DISPATCH_RING,)),
                            pltpu.SemaphoreType.DMA(())],
        ),
        out_shape=jax.ShapeDtypeStruct((xd_rows, LANES), F32),
        compiler_params=pltpu.CompilerParams(
            dimension_semantics=("arbitrary",), vmem_limit_bytes=VMEM_LIMIT),
        name="moe_dispatch",
    )(dest8, pad_lo, pad_hi, h2t)


def _moe_kernel(blk_e_ref, nused_ref, xd_hbm, wg_ref, wu_ref, wd_ref, yd_ref, wg_s, wu_s, wd_s,
                xbuf, xsem):
    i = pl.program_id(0)
    nused = nused_ref[0]
    blk_rows = MOE_BLK * ROW_SUB

    def block_copy(j):
        src = pl.multiple_of(j * blk_rows, blk_rows)
        return pltpu.make_async_copy(xd_hbm.at[pl.ds(src, blk_rows)], xbuf.at[j % MOE_RING],
                                     xsem.at[j % MOE_RING])

    for j in range(MOE_RING - 1):
        @pl.when((i == 0) & (j < nused))
        def _prime():
            block_copy(j).start()

    @pl.when(i + MOE_RING - 1 < nused)
    def _prefetch():
        block_copy(i + MOE_RING - 1).start()

    @pl.when((i == 0) | (blk_e_ref[i] != blk_e_ref[jnp.maximum(i - 1, 0)]))
    def _cast_weights():
        wg_s[...] = wg_ref[0, 0].astype(BF16)
        wu_s[...] = wu_ref[0, 0].astype(BF16)
        wd_s[...] = wd_ref[0, 0].astype(BF16)

    @pl.when(i < nused)
    def _compute():
        block_copy(i).wait()
        xb = _load_row_tiles(xbuf.at[i % MOE_RING], MOE_BLK).astype(BF16)
        gate = _dot(xb, wg_s[...])
        up = _dot(xb, wu_s[...])
        hid = (gate * _sigmoid(gate)) * up
        _store_row_tiles(yd_ref, _dot(hid.astype(BF16), wd_s[...]))

    @pl.when(i >= nused)
    def _unused():
        yd_ref[...] = jnp.zeros_like(yd_ref)


def _moe(blk_e, nused, xd, wg, wu, wd, layer):
    nb = blk_e.shape[0]
    d = wg.shape[2]
    blk_rows = MOE_BLK * ROW_SUB

    def w_spec(shape):
        return pl.BlockSpec(shape, lambda i, be, nu: (layer, be[i], 0, 0))

    return pl.pallas_call(
        _moe_kernel,
        grid_spec=pltpu.PrefetchScalarGridSpec(
            num_scalar_prefetch=2,
            grid=(nb,),
            in_specs=[
                pl.BlockSpec(memory_space=pl.ANY),
                w_spec((1, 1, d, D_EXPERT)),
                w_spec((1, 1, d, D_EXPERT)),
                w_spec((1, 1, D_EXPERT, d)),
            ],
            out_specs=pl.BlockSpec((blk_rows, LANES), lambda i, be, nu: (i, 0)),
            scratch_shapes=[pltpu.VMEM((d, D_EXPERT), BF16), pltpu.VMEM((d, D_EXPERT), BF16),
                            pltpu.VMEM((D_EXPERT, d), BF16),
                            pltpu.VMEM((MOE_RING, blk_rows, LANES), F32),
                            pltpu.SemaphoreType.DMA((MOE_RING,))],
        ),
        out_shape=jax.ShapeDtypeStruct((nb * blk_rows, LANES), F32),
        compiler_params=pltpu.CompilerParams(
            dimension_semantics=("arbitrary",), vmem_limit_bytes=VMEM_LIMIT),
        name="moe_experts",
    )(blk_e, nused, xd, wg, wu, wd)


def _combine_kernel(dest_ref, x_ref, mod_ref, g_ref, rt_ref, yd_hbm, o_ref, buf, sem, *, final, tile_of):
    s = pl.program_id(0)
    n_steps = pl.num_programs(0)
    tm = x_ref.shape[0]
    slot = s % 2

    def start_gather(step, dst_slot):
        base = tile_of(step) * tm

        def body(r, carry):
            for k in range(2):
                src = pl.multiple_of(dest_ref[(base + r) * 2 + k], ROW_SUB)
                dst = pl.multiple_of(r * ROW_SUB, ROW_SUB)
                pltpu.make_async_copy(yd_hbm.at[pl.ds(src, ROW_SUB)], buf.at[dst_slot, k, pl.ds(dst, ROW_SUB)],
                                      sem.at[dst_slot]).start(priority=k)
            return carry
        lax.fori_loop(0, tm, body, 0, unroll=DMA_UNROLL)

    @pl.when(s == 0)
    def _prime():
        start_gather(0, 0)

    @pl.when(s + 1 < n_steps)
    def _prefetch():
        start_gather(s + 1, 1 - slot)

    for k in range(2):
        pltpu.make_async_copy(yd_hbm.at[pl.ds(0, tm * ROW_SUB)], buf.at[slot, k], sem.at[slot]).wait()
    w0 = rt_ref[:, 2:3]
    w1 = rt_ref[:, 3:4]
    y = w0 * _load_row_tiles(buf.at[slot, 0], tm) + w1 * _load_row_tiles(buf.at[slot, 1], tm)
    x_new = x_ref[...] + mod_ref[0, 5:6, :] * y
    if final:
        ms = jnp.mean(x_new * x_new, axis=-1, keepdims=True)
        x_new = x_new * lax.rsqrt(ms + EPS) * g_ref[...]
    o_ref[...] = x_new


def _combine(dest, x_flat, mod8, g_final, rt, yd, n_lat, n_tiles, final):
    ntok, d = x_flat.shape
    tm = TOK_TILE
    bsz = ntok // (tm * n_tiles)
    n_per = n_lat if final else n_tiles
    n_steps = bsz * n_per

    def tile_of(step):
        return (step // n_per) * n_tiles + step % n_per

    def mod_map(s, dref):
        return (jnp.where(s % n_per < n_lat, s // n_per, bsz), 0, 0)

    return pl.pallas_call(
        functools.partial(_combine_kernel, final=final, tile_of=tile_of),
        grid_spec=pltpu.PrefetchScalarGridSpec(
            num_scalar_prefetch=1,
            grid=(n_steps,),
            in_specs=[
                pl.BlockSpec((tm, d), lambda s, dref: (tile_of(s), 0)),
                pl.BlockSpec((1, 8, d), mod_map),
                pl.BlockSpec((1, d), lambda s, dref: (0, 0)),
                pl.BlockSpec((tm, LANES), lambda s, dref: (tile_of(s), 0)),
                pl.BlockSpec(memory_space=pl.ANY),
            ],
            out_specs=pl.BlockSpec((tm, d), lambda s, dref: (s, 0)),
            scratch_shapes=[pltpu.VMEM((2, 2, tm * ROW_SUB, LANES), F32), pltpu.SemaphoreType.DMA((2,))],
        ),
        out_shape=jax.ShapeDtypeStruct((n_steps * tm, d), F32),
        compiler_params=pltpu.CompilerParams(
            dimension_semantics=("arbitrary",), vmem_limit_bytes=VMEM_LIMIT),
        name="combine",
    )(dest, x_flat, mod8, g_final, rt, yd)


def _rope_tables(seq, ctx_len):
    quarter = HEAD_DIM // 4
    rows = seq // GRID_W
    inv = ROPE_BASE ** (-jnp.arange(quarter, dtype=F32) / quarter)
    ang_r = jnp.arange(rows, dtype=F32)[:, None] * inv[None, :]
    ang_c = jnp.arange(GRID_W, dtype=F32)[:, None] * inv[None, :]

    def by_row(a):
        return jnp.broadcast_to(a[:, None, :], (rows, GRID_W, quarter)).reshape(seq, quarter)

    def by_col(a):
        return jnp.broadcast_to(a[None, :, :], (rows, GRID_W, quarter)).reshape(seq, quarter)

    cos_r, sin_r = by_row(jnp.cos(ang_r)), by_row(jnp.sin(ang_r))
    cos_c, sin_c = by_col(jnp.cos(ang_c)), by_col(jnp.sin(ang_c))
    cos = jnp.concatenate([cos_r, cos_r, cos_c, cos_c], axis=-1)
    sin = jnp.concatenate([-sin_r, sin_r, -sin_c, sin_c], axis=-1)
    cos = jnp.concatenate([cos, jnp.ones((ctx_len, HEAD_DIM), F32)], axis=0)
    sin = jnp.concatenate([sin, jnp.zeros((ctx_len, HEAD_DIM), F32)], axis=0)
    return jnp.concatenate([cos, cos], axis=-1), jnp.concatenate([sin, sin], axis=-1)


def _na_bias_tables(rpb, rows):
    qc = np.arange(GRID_W)
    kc = np.arange(GRID_W)
    cstart = np.clip(qc - WIN_C // 2, 0, GRID_W - WIN_C)
    rel_c = np.clip(kc[None, :] - qc[:, None] + WIN_C - 1, 0, 2 * WIN_C - 2)
    col_ok = (kc[None, :] >= cstart[:, None]) & (kc[None, :] < cstart[:, None] + WIN_C)
    t = jnp.where(col_ok[None, None], rpb[:, :, rel_c], NEG_INF)
    masked = jnp.full((rpb.shape[0], GRID_W, GRID_W), NEG_INF, F32)
    n_blk = rows // NA_ROWS
    cases = []
    for blk in (0, min(1, n_blk - 1), n_blk - 1):
        r0 = blk * NA_ROWS
        ks = int(np.clip(r0 - WIN_R // 2, 0, rows - NA_KROWS))
        q_rows = []
        for qr in range(NA_ROWS):
            r = r0 + qr
            rs = int(np.clip(r - WIN_R // 2, 0, rows - WIN_R))
            tiles = []
            for a in range(NA_KROWS):
                kr = ks + a
                tiles.append(t[:, kr - r + WIN_R - 1] if rs <= kr < rs + WIN_R else masked)
            q_rows.append(jnp.concatenate(tiles, axis=-1))
        cases.append(jnp.concatenate(q_rows, axis=-2))
    return jnp.stack(cases, axis=0).astype(F32)


def _routing_plan(rt, cnt, n_blocks):
    counts = cnt[0, 8:8 + N_EXPERTS].astype(jnp.int32)
    padded = ((counts + MOE_BLK - 1) // MOE_BLK) * MOE_BLK
    pad_end = jnp.cumsum(padded)
    pad_start = pad_end - padded
    eid = rt[:, 0:2].astype(jnp.int32)
    pos = rt[:, 4:6].astype(jnp.int32)
    start_of = jnp.sum(jnp.where(eid[:, :, None] == jnp.arange(N_EXPERTS)[None, None, :],
                                 pad_start[None, None, :], 0), axis=-1)
    dest8 = ((start_of + pos) * ROW_SUB).reshape(-1)
    blk_start = jnp.arange(n_blocks, dtype=jnp.int32) * MOE_BLK
    blk_e = jnp.clip(jnp.sum(pad_end[None, :] <= blk_start[:, None], axis=1), 0, N_EXPERTS - 1).astype(jnp.int32)
    nused = (pad_end[-1] // MOE_BLK).astype(jnp.int32).reshape(1)
    total = jnp.full((1,), n_blocks * MOE_BLK, jnp.int32)
    pad_lo = jnp.concatenate([pad_start + counts, pad_end[-1:]]).astype(jnp.int32)
    pad_hi = jnp.concatenate([pad_end, total]).astype(jnp.int32)
    return blk_e, nused, dest8, pad_lo, pad_hi


def kernel(x, c, ctx, c_ctx, w_mod, b_mod, g_attn, g_ffn, w_in, conv_w, rpb, sinks, w_branch, w_o,
           w_rg, b_rg, w_re, b_re, w_e_gate, w_e_up, w_e_down, g_final):
    bsz, seq, d = x.shape
    ctx_len = ctx.shape[1]
    depth = w_in.shape[0]
    ltot = seq + ctx_len
    n_lat = seq // TOK_TILE
    n_tiles = ltot // TOK_TILE
    rows = seq // GRID_W
    ntok = bsz * ltot
    n_assign = ntok * 2
    n_blocks = (n_assign + N_EXPERTS * (MOE_BLK - 1) + MOE_BLK - 1) // MOE_BLK

    x_all = jnp.concatenate([x, ctx], axis=1)
    cos_t, sin_t = _rope_tables(seq, ctx_len)
    assert bsz + 1 <= 8
    c8 = jnp.pad(jnp.concatenate([c, c_ctx[None, :]], axis=0), ((0, 7 - bsz), (0, 0)))
    mod_all = _mod_vectors(c8, w_mod, b_mod)
    w_in_bf = w_in.astype(BF16)
    w_branch_bf = w_branch.astype(BF16)
    w_o_bf = w_o.astype(BF16)

    for l in range(depth):
        mod8 = jnp.pad(mod_all[l, :bsz + 1].reshape(bsz + 1, 6, d), ((0, 0), (0, 2), (0, 0)))
        w_l = w_in_bf[l]
        bias = _na_bias_tables(rpb[l], rows)
        o2 = D_CONV + D_NA
        wb = jnp.concatenate([w_branch_bf[l, :o2]] + [w_branch_bf[l, o2 + h * HEAD_DIM:o2 + (h + 1) * HEAD_DIM]
                                                      for h in _SW_HEAD_ORDER], axis=0)
        pad_g = jnp.zeros((d, 8 - N_GROUPS), F32)
        pad_e = jnp.zeros((d, LANES - 8 - N_EXPERTS), F32)
        wr = jnp.concatenate([w_rg[l], pad_g, w_re[l], pad_e], axis=1)
        wr_hi = wr.astype(BF16)
        wr_lo = (wr - wr_hi.astype(F32)).astype(BF16)
        br = jnp.concatenate([b_rg[l], pad_g[0], b_re[l], pad_e[0]])[None, :]
        convw8 = jnp.pad(conv_w[l], ((0, 8 - CONV_W), (0, 0)))

        z, ab, nq, nk, nv, wq, wk, wv, gates = _inproj(
            x_all, mod8, g_attn[l][None, :], w_l, cos_t, sin_t, n_lat)
        y_b = _na_attn(nq, nk, nv, bias, seq)
        y_c = _sw_attn(sinks[l], wq, wk, wv, seq)
        x_mid, h2t, rt, cnt = _merge(x_all, mod8, g_ffn[l][None, :], z, ab, y_b, y_c, gates, convw8, wb,
                                     w_o_bf[l], wr_hi, wr_lo, br, n_lat)
        rt = rt.reshape(ntok, LANES)
        blk_e, nused, dest8, pad_lo, pad_hi = _routing_plan(rt, cnt, n_blocks)
        xd = _dispatch(dest8, pad_lo, pad_hi, h2t, n_blocks)
        yd = _moe(blk_e, nused, xd, w_e_gate, w_e_up, w_e_down, l)
        final = l == depth - 1
        x_flat = _combine(dest8, x_mid.reshape(ntok, d), mod8, g_final[None, :], rt, yd, n_lat, n_tiles,
                          final)
        if not final:
            x_all = x_flat.reshape(bsz, ltot, d)
    return x_flat.reshape(bsz, seq, d)
```

```python
import functools

import numpy as np
import jax
import jax.numpy as jnp
from jax import lax
from jax.experimental import pallas as pl
from jax.experimental.pallas import tpu as pltpu

F32 = jnp.float32
BF16 = jnp.bfloat16

D_MODEL = 1024
GRID_W = 64
HEAD_DIM = 64
D_CONV = 256
CONV_W = 3
H_NA = 6
WIN_R = 8
WIN_C = 16
H_Q = 6
H_KV = 2
WINDOW = 128
ROPE_BASE = 10000.0
D_NA = H_NA * HEAD_DIM
D_SW = H_Q * HEAD_DIM
D_KV = H_KV * HEAD_DIM
N_GROUPS = 4
EXP_PER_GROUP = 8
N_EXPERTS = N_GROUPS * EXP_PER_GROUP
D_EXPERT = 256
MOE_BLK = 256
EPS = 1e-6
NEG_INF = -1e30

LANES = 128
ROW_SUB = D_MODEL // LANES
TOK_TILE = 256
INPROJ_SUB = 2
NA_ROWS = TOK_TILE // GRID_W
NA_KROWS = NA_ROWS + WIN_R - 1
NA_KEYS = NA_KROWS * GRID_W
SW_KEYS = TOK_TILE + 2 * WINDOW
VMEM_LIMIT = 56 * 1024 * 1024
DMA_UNROLL = 8
MOE_PER_STEP = 2
MOE_RING = 6
DISPATCH_RING = 3

_C_CONV = 0
_C_NQ = _C_CONV + 3 * D_CONV
_C_NK = _C_NQ + D_NA
_C_NV = _C_NK + D_NA
_C_WQ = _C_NV + D_NA
_C_WK = _C_WQ + D_SW
_C_WV = _C_WK + D_KV
_C_GATE = _C_WV + D_KV
_C_END = _C_GATE + 3 * D_MODEL

_SW_HEAD_ORDER = (0, 3, 1, 4, 2, 5)


def _dot(a, b):
    return jnp.dot(a, b, preferred_element_type=F32)


def _dot_nt(a, b):
    return lax.dot_general(a, b, (((1,), (1,)), ((), ())), preferred_element_type=F32)


def _sigmoid(x):
    return 1.0 / (1.0 + jnp.exp(-x))


def _modln(x, g, shift, scale):
    ms = jnp.mean(x * x, axis=-1, keepdims=True)
    return (x * lax.rsqrt(ms + EPS)) * g * (1.0 + scale) + shift


def _store_row_tiles(ref, val):
    n = val.shape[0]
    for s in range(ROW_SUB):
        ref[pl.ds(s, n, stride=ROW_SUB), :] = val[:, s * LANES:(s + 1) * LANES]


def _load_row_tiles(ref, n):
    return jnp.concatenate([ref[pl.ds(s, n, stride=ROW_SUB), :] for s in range(ROW_SUB)], axis=-1)


def _mod_kernel(c_ref, w_ref, b_ref, o_ref):
    cv = c_ref[...]
    sc = (cv * _sigmoid(cv)).astype(BF16)
    o_ref[0] = _dot(sc, w_ref[0].astype(BF16)) + b_ref[0]


def _mod_vectors(c8, w_mod, b_mod):
    depth, d, cols = w_mod.shape
    tn = cols // 4
    return pl.pallas_call(
        _mod_kernel,
        grid=(depth, cols // tn),
        in_specs=[
            pl.BlockSpec((8, d), lambda l, j: (0, 0)),
            pl.BlockSpec((1, d, tn), lambda l, j: (l, 0, j)),
            pl.BlockSpec((1, 1, tn), lambda l, j: (l, 0, j)),
        ],
        out_specs=pl.BlockSpec((1, 8, tn), lambda l, j: (l, 0, j)),
        out_shape=jax.ShapeDtypeStruct((depth, 8, cols), F32),
        compiler_params=pltpu.CompilerParams(
            dimension_semantics=("arbitrary", "arbitrary"), vmem_limit_bytes=VMEM_LIMIT),
        name="mod_vectors",
    )(c8, w_mod, b_mod.reshape(depth, 1, cols))


def _inproj_kernel(*refs):
    ns = INPROJ_SUB
    x_ref, g_ref, w_ref = refs[0], refs[1], refs[2]
    mod_refs = refs[3:3 + ns]
    cos_refs = refs[3 + ns:3 + 2 * ns]
    sin_refs = refs[3 + 2 * ns:3 + 3 * ns]
    z_ref, ab_ref, nq_ref, nk_ref, nv_ref, wq_ref, wk_ref, wv_ref, gate_ref = refs[3 + 3 * ns:]

    h = jnp.concatenate(
        [_modln(x_ref[j * TOK_TILE:(j + 1) * TOK_TILE, :], g_ref[...],
                mod_refs[j][0, 0:1, :], mod_refs[j][0, 1:2, :]).astype(BF16) for j in range(ns)], axis=0)

    def proj(c0, c1):
        return _dot(h, w_ref[0, :, c0:c1])

    a_h = proj(_C_CONV, _C_CONV + D_CONV)
    a_b = proj(_C_CONV + D_CONV, _C_CONV + 2 * D_CONV)
    a_c = proj(_C_CONV + 2 * D_CONV, _C_NQ)
    z_ref[...] = (a_c * a_h).astype(BF16)
    ab_ref[...] = a_b.astype(BF16)

    qk_scale = HEAD_DIM ** -0.5
    nq_ref[...] = (proj(_C_NQ, _C_NK) * qk_scale).astype(BF16)
    nk_ref[...] = proj(_C_NK, _C_NV).astype(BF16)
    nv_ref[...] = proj(_C_NV, _C_WQ).astype(BF16)

    cos = jnp.concatenate([r[...] for r in cos_refs], axis=0)
    sin = jnp.concatenate([r[...] for r in sin_refs], axis=0)
    cos3 = jnp.concatenate([cos] * (D_SW // LANES), axis=-1)
    sin3 = jnp.concatenate([sin] * (D_SW // LANES), axis=-1)

    def rope(v, cos_v, sin_v):
        quarter = HEAD_DIM // 4
        width = v.shape[-1]
        lane = lax.broadcasted_iota(jnp.int32, (1, width), 1)
        first = (lane % (2 * quarter)) < quarter
        swapped = jnp.where(first, pltpu.roll(v, width - quarter, axis=1), pltpu.roll(v, quarter, axis=1))
        return v * cos_v + swapped * sin_v

    wq = rope(proj(_C_WQ, _C_WK), cos3, sin3) * qk_scale
    wq = jnp.concatenate([wq[:, h * HEAD_DIM:(h + 1) * HEAD_DIM] for h in _SW_HEAD_ORDER], axis=1)
    wq_ref[...] = wq.astype(BF16)
    wk_ref[...] = rope(proj(_C_WK, _C_WV), cos, sin).astype(BF16)
    wv_ref[...] = proj(_C_WV, _C_GATE).astype(BF16)

    for j in range(3):
        c0 = _C_GATE + j * D_MODEL
        gate_ref[:, j * D_MODEL:(j + 1) * D_MODEL] = _sigmoid(proj(c0, c0 + D_MODEL)).astype(BF16)


def _inproj(x_all, mod8, g, w_all, layer, cos_t, sin_t, n_lat):
    bsz, ltot, d = x_all.shape
    n_tiles = ltot // TOK_TILE
    ns = INPROJ_SUB
    tm = ns * TOK_TILE
    ntok = bsz * ltot
    assert ntok % tm == 0

    def tok_spec(width):
        return pl.BlockSpec((tm, width), lambda t: (t, 0))

    def mod_spec(j):
        def index(t):
            u = t * ns + j
            return (jnp.where(u % n_tiles < n_lat, u // n_tiles, bsz), 0, 0)
        return pl.BlockSpec((1, 8, d), index)

    def rope_spec(j):
        return pl.BlockSpec((TOK_TILE, LANES), lambda t: ((t * ns + j) % n_tiles, 0))

    widths = (D_CONV, D_CONV, D_NA, D_NA, D_NA, D_SW, D_KV, D_KV, 3 * D_MODEL)
    outs = pl.pallas_call(
        _inproj_kernel,
        grid=(ntok // tm,),
        in_specs=([tok_spec(d),
                   pl.BlockSpec((1, d), lambda t: (0, 0)),
                   pl.BlockSpec((1, d, _C_END), lambda t: (layer, 0, 0), pipeline_mode=pl.Buffered(1))]
                  + [mod_spec(j) for j in range(ns)]
                  + [rope_spec(j) for j in range(ns)]
                  + [rope_spec(j) for j in range(ns)]),
        out_specs=[tok_spec(wd) for wd in widths],
        out_shape=[jax.ShapeDtypeStruct((ntok, wd), BF16) for wd in widths],
        compiler_params=pltpu.CompilerParams(
            dimension_semantics=("arbitrary",), vmem_limit_bytes=VMEM_LIMIT),
        name="inproj",
    )(x_all.reshape(ntok, d), g, w_all, *([mod8] * ns), *([cos_t] * ns), *([sin_t] * ns))
    return [o.reshape(bsz, ltot, o.shape[-1]) for o in outs]


def _half_mask(half):
    lane = lax.broadcasted_iota(jnp.int32, (1, LANES), 1)
    return (lane < HEAD_DIM) if half == 0 else (lane >= HEAD_DIM)


def _softmax_pv(parts, sink=None):
    m = parts[0][0].max(axis=-1, keepdims=True)
    for s, _ in parts[1:]:
        m = jnp.maximum(m, s.max(axis=-1, keepdims=True))
    if sink is not None:
        m = jnp.maximum(m, sink)
    l = None
    o = None
    for s, v in parts:
        e = jnp.exp(s - m)
        ls = e.sum(axis=-1, keepdims=True)
        os_ = _dot(e.astype(BF16), v)
        l = ls if l is None else l + ls
        o = os_ if o is None else o + os_
    if sink is not None:
        l = l + jnp.exp(sink - m)
    return o * (1.0 / l)


def _na_kernel(q_ref, k_ref, v_ref, bias_ref, o_ref, *, n_blk, rows, seq):
    i = pl.program_id(1)
    ctx_len = k_ref.shape[1] - seq

    def pair_slices(p):
        return slice(p * LANES, (p + 1) * LANES)

    @pl.when(i < n_blk)
    def _latent():
        r0 = i * NA_ROWS
        ks = jnp.clip(r0 - WIN_R // 2, 0, rows - NA_KROWS)
        kstart = pl.multiple_of(ks * GRID_W, GRID_W)
        for p in range(H_NA // 2):
            ps = pair_slices(p)
            q2 = q_ref[0, :, ps]
            k2 = k_ref[0, pl.ds(kstart, NA_KEYS), ps]
            v2 = v_ref[0, pl.ds(kstart, NA_KEYS), ps]
            kc2 = k_ref[0, seq:seq + ctx_len, ps]
            vc2 = v_ref[0, seq:seq + ctx_len, ps]
            qm = jnp.concatenate([jnp.where(_half_mask(half), q2, jnp.zeros_like(q2)) for half in range(2)],
                                 axis=0)
            bias2 = jnp.concatenate([bias_ref[0, 2 * p], bias_ref[0, 2 * p + 1]], axis=0)
            o2 = _softmax_pv([(_dot_nt(qm, k2) + bias2, v2), (_dot_nt(qm, kc2), vc2)])
            o_ref[0, :, ps] = jnp.where(_half_mask(0), o2[:TOK_TILE], o2[TOK_TILE:]).astype(BF16)

    @pl.when(i == n_blk)
    def _context():
        for p in range(H_NA // 2):
            ps = pair_slices(p)
            q2 = q_ref[0, :, ps]
            kc2 = k_ref[0, seq:seq + ctx_len, ps]
            vc2 = v_ref[0, seq:seq + ctx_len, ps]
            outs = []
            for half in range(2):
                qm = jnp.where(_half_mask(half), q2, jnp.zeros_like(q2))
                outs.append(_softmax_pv([(_dot_nt(qm, kc2), vc2)]))
            o_ref[0, :, ps] = jnp.where(_half_mask(0), outs[0], outs[1]).astype(BF16)


def _na_attn(nq, nk, nv, bias, seq):
    bsz, ltot, _ = nq.shape
    rows = seq // GRID_W
    n_blk = seq // TOK_TILE
    n_steps = ltot // TOK_TILE

    def bias_map(b, i):
        return (jnp.where(i == 0, 0, jnp.where(i >= n_blk - 1, 2, 1)), 0, 0, 0)

    return pl.pallas_call(
        functools.partial(_na_kernel, n_blk=n_blk, rows=rows, seq=seq),
        grid=(bsz, n_steps),
        in_specs=[
            pl.BlockSpec((1, TOK_TILE, D_NA), lambda b, i: (b, i, 0)),
            pl.BlockSpec((1, ltot, D_NA), lambda b, i: (b, 0, 0)),
            pl.BlockSpec((1, ltot, D_NA), lambda b, i: (b, 0, 0)),
            pl.BlockSpec((1, H_NA, TOK_TILE, NA_KEYS), bias_map),
        ],
        out_specs=pl.BlockSpec((1, TOK_TILE, D_NA), lambda b, i: (b, i, 0)),
        out_shape=jax.ShapeDtypeStruct((bsz, ltot, D_NA), BF16),
        compiler_params=pltpu.CompilerParams(
            dimension_semantics=("arbitrary", "arbitrary"), vmem_limit_bytes=VMEM_LIMIT),
        name="na_attn",
    )(nq, nk, nv, bias)


def _sw_kernel(sink_ref, q_ref, k_ref, v_ref, o_ref, *, n_blk, seq):
    i = pl.program_id(1)
    ctx_len = k_ref.shape[1] - seq
    kc2 = k_ref[0, seq:seq + ctx_len, :]
    vc2 = v_ref[0, seq:seq + ctx_len, :]

    row2 = lax.broadcasted_iota(jnp.int32, (2 * TOK_TILE, 1), 0)

    def heads_of_pair(p, window):
        q2 = q_ref[0, :, p * LANES:(p + 1) * LANES]
        qm = jnp.concatenate([jnp.where(_half_mask(half), q2, jnp.zeros_like(q2)) for half in range(2)], axis=0)
        sink = jnp.where(row2 < TOK_TILE, sink_ref[p], sink_ref[H_Q // H_KV + p])
        parts = []
        if window is not None:
            k2, v2, valid = window
            parts.append((jnp.where(valid, _dot_nt(qm, k2), NEG_INF), v2))
        parts.append((_dot_nt(qm, kc2), vc2))
        o2 = _softmax_pv(parts, sink=sink)
        o_ref[0, :, p * LANES:(p + 1) * LANES] = jnp.where(_half_mask(0), o2[:TOK_TILE], o2[TOK_TILE:]).astype(BF16)

    @pl.when(i < n_blk)
    def _latent():
        t0 = i * TOK_TILE
        kst = pl.multiple_of(jnp.clip(t0 - WINDOW, 0, seq - SW_KEYS), WINDOW)
        qpos = t0 + row2 % TOK_TILE
        kpos = kst + lax.broadcasted_iota(jnp.int32, (1, SW_KEYS), 1)
        valid = jnp.abs(qpos - kpos) <= WINDOW
        window = (k_ref[0, pl.ds(kst, SW_KEYS), :], v_ref[0, pl.ds(kst, SW_KEYS), :], valid)
        for p in range(H_Q // 2):
            heads_of_pair(p, window)

    @pl.when(i == n_blk)
    def _context():
        for p in range(H_Q // 2):
            heads_of_pair(p, None)


def _sw_attn(sinks, wq, wk, wv, seq):
    bsz, ltot, _ = wq.shape
    n_blk = seq // TOK_TILE
    n_steps = ltot // TOK_TILE
    return pl.pallas_call(
        functools.partial(_sw_kernel, n_blk=n_blk, seq=seq),
        grid_spec=pltpu.PrefetchScalarGridSpec(
            num_scalar_prefetch=1,
            grid=(bsz, n_steps),
            in_specs=[
                pl.BlockSpec((1, TOK_TILE, D_SW), lambda b, i, s: (b, i, 0)),
                pl.BlockSpec((1, ltot, D_KV), lambda b, i, s: (b, 0, 0)),
                pl.BlockSpec((1, ltot, D_KV), lambda b, i, s: (b, 0, 0)),
            ],
            out_specs=pl.BlockSpec((1, TOK_TILE, D_SW), lambda b, i, s: (b, i, 0)),
        ),
        out_shape=jax.ShapeDtypeStruct((bsz, ltot, D_SW), BF16),
        compiler_params=pltpu.CompilerParams(
            dimension_semantics=("arbitrary", "arbitrary"), vmem_limit_bytes=VMEM_LIMIT),
        name="sw_attn",
    )(sinks, wq, wk, wv)


def _split_bf16(x):
    hi = x.astype(BF16)
    lo = (x - hi.astype(F32)).astype(BF16)
    return hi, lo


def _merge_kernel(x_ref, mod_ref, g_ref, z_ref, zp_ref, zn_ref, ab_ref, yb_ref, yc_ref, gate_ref,
                  convw_ref, wb_ref, wo_ref, wr_ref, br_ref,
                  xo_ref, h2_ref, rt_ref, rtt_ref, cnt_ref, *, n_lat):
    i = pl.program_id(1)
    n_tiles = pl.num_programs(1)
    tm = x_ref.shape[1]

    z = z_ref[0].astype(F32)
    halo = zp_ref.shape[1]
    first = (i == 0) | (i == n_lat)
    last = (i == n_lat - 1) | (i == n_tiles - 1)
    prev_row = jnp.where(first, 0.0, zp_ref[0, halo - 1:halo, :].astype(F32))
    next_row = jnp.where(last, 0.0, zn_ref[0, 0:1, :].astype(F32))
    row = lax.broadcasted_iota(jnp.int32, (tm, 1), 0)
    z_prev = jnp.where(row == 0, prev_row, pltpu.roll(z, 1, axis=0))
    z_next = jnp.where(row == tm - 1, next_row, pltpu.roll(z, tm - 1, axis=0))
    cw = convw_ref[...]
    y_a = ab_ref[0].astype(F32) * (z_prev * cw[0:1] + z * cw[1:2] + z_next * cw[2:3])

    o1 = D_CONV
    o2 = D_CONV + D_NA
    g_a = gate_ref[0, :, 0:D_MODEL].astype(F32)
    g_b = gate_ref[0, :, D_MODEL:2 * D_MODEL].astype(F32)
    g_c = gate_ref[0, :, 2 * D_MODEL:3 * D_MODEL].astype(F32)
    m = (g_a * _dot(y_a.astype(BF16), wb_ref[0:o1, :])
         + g_b * _dot(yb_ref[0], wb_ref[o1:o2, :])
         + g_c * _dot(yc_ref[0], wb_ref[o2:, :]))
    y = _dot(m.astype(BF16), wo_ref[...])
    x_new = x_ref[0] + mod_ref[0, 2:3, :] * y
    xo_ref[0] = x_new

    h2 = _modln(x_new, g_ref[...], mod_ref[0, 3:4, :], mod_ref[0, 4:5, :])
    _store_row_tiles(h2_ref, h2)

    h_hi, h_lo = _split_bf16(h2)
    prod = _dot(jnp.concatenate([h_hi, h_lo], axis=0), wr_ref[...])
    lg = prod[:tm, :LANES] + prod[:tm, LANES:] + prod[tm:, :LANES] + br_ref[...]
    lane = lax.broadcasted_iota(jnp.int32, lg.shape, 1)
    lane_f = lane.astype(F32)
    big = float(LANES)

    def masked_argmax(v):
        vmax = v.max(axis=-1, keepdims=True)
        idx = jnp.where(v == vmax, lane_f, big).min(axis=-1, keepdims=True)
        return vmax, idx

    is_grp = lane < N_GROUPS
    g_max, g_idx = masked_argmax(jnp.where(is_grp, lg, -jnp.inf))
    pg_top = 1.0 / jnp.where(is_grp, jnp.exp(lg - g_max), 0.0).sum(axis=-1, keepdims=True)
    e_lo = 8.0 + g_idx * EXP_PER_GROUP
    cand = jnp.where(lane_f >= e_lo, jnp.where(lane_f < e_lo + EXP_PER_GROUP, lg, -jnp.inf), -jnp.inf)
    v1, i1 = masked_argmax(cand)
    v2, i2 = masked_argmax(jnp.where(lane_f == i1, -jnp.inf, cand))
    e21 = jnp.exp(v2 - v1)
    w1 = pg_top / (1.0 + e21)
    w2 = pg_top * e21 / (1.0 + e21)

    @pl.when((pl.program_id(0) == 0) & (i == 0))
    def _init_counts():
        cnt_ref[...] = jnp.zeros_like(cnt_ref)

    oh1 = jnp.where(lane_f == i1, 1.0, 0.0)
    oh2 = jnp.where(lane_f == i2, 1.0, 0.0)
    ohs = oh1 + oh2
    tri = (lax.broadcasted_iota(jnp.int32, (tm, tm), 0) > lax.broadcasted_iota(jnp.int32, (tm, tm), 1))
    before = _dot(tri.astype(BF16), ohs.astype(BF16)) + cnt_ref[0:1, :]
    pos1 = (before * oh1).sum(axis=-1, keepdims=True)
    pos2 = (before * oh2).sum(axis=-1, keepdims=True)
    cnt_ref[...] = cnt_ref[...] + ohs.sum(axis=0, keepdims=True)

    vals = (i1 - 8.0, i2 - 8.0, w1, w2, pos1, pos2)
    rt = jnp.zeros_like(lg)
    for j, v in enumerate(vals):
        rt = jnp.where(lane == j, v, rt)
    rt_ref[0] = rt
    rtt_ref[...] = jnp.transpose(rt)[0:8, :]


def _merge(x_all, mod8, g, z, ab, y_b, y_c, gates, conv_w, wb, wo, wr, br, n_lat):
    bsz, ltot, d = x_all.shape
    n_tiles = ltot // TOK_TILE
    tm = TOK_TILE
    halo = 16
    hb = tm // halo
    n_halo = ltot // halo

    def tok_spec(width):
        return pl.BlockSpec((1, tm, width), lambda b, i: (b, i, 0))

    def mod_map(b, i):
        return (jnp.where(i < n_lat, b, bsz), 0, 0)

    def const2(shape):
        return pl.BlockSpec(shape, lambda b, i: (0, 0))

    return pl.pallas_call(
        functools.partial(_merge_kernel, n_lat=n_lat),
        grid=(bsz, n_tiles),
        in_specs=[
            tok_spec(d),
            pl.BlockSpec((1, 8, d), mod_map),
            const2((1, d)),
            tok_spec(D_CONV),
            pl.BlockSpec((1, halo, D_CONV), lambda b, i: (b, jnp.maximum(i * hb - 1, 0), 0)),
            pl.BlockSpec((1, halo, D_CONV), lambda b, i: (b, jnp.minimum((i + 1) * hb, n_halo - 1), 0)),
            tok_spec(D_CONV),
            tok_spec(D_NA),
            tok_spec(D_SW),
            tok_spec(3 * d),
            const2((8, D_CONV)),
            const2((d, d)),
            const2((d, d)),
            const2((d, 2 * LANES)),
            const2((1, LANES)),
        ],
        out_specs=[tok_spec(d),
                   pl.BlockSpec((tm * ROW_SUB, LANES), lambda b, i: (b * n_tiles + i, 0)),
                   tok_spec(LANES),
                   pl.BlockSpec((8, tm), lambda b, i: (0, b * n_tiles + i)),
                   const2((8, LANES))],
        out_shape=[jax.ShapeDtypeStruct((bsz, ltot, d), F32),
                   jax.ShapeDtypeStruct((bsz * ltot * ROW_SUB, LANES), F32),
                   jax.ShapeDtypeStruct((bsz, ltot, LANES), F32),
                   jax.ShapeDtypeStruct((8, bsz * ltot), F32),
                   jax.ShapeDtypeStruct((8, LANES), F32)],
        compiler_params=pltpu.CompilerParams(
            dimension_semantics=("arbitrary", "arbitrary"), vmem_limit_bytes=VMEM_LIMIT),
        name="merge",
    )(x_all, mod8, g, z, z, z, ab, y_b, y_c, gates, conv_w, wb, wo, wr, br)


def _dispatch_kernel(dest_ref, pad_lo_ref, pad_hi_ref, h2_hbm, xd_hbm, hbuf, zero_buf, in_sem, out_sem, zsem,
                     *, pads_per_step):
    s = pl.program_id(0)
    n_steps = pl.num_programs(0)
    tm = hbuf.shape[1] // ROW_SUB
    tile_rows = tm * ROW_SUB
    ntok = dest_ref.shape[0] // 2

    def tile_copy(t):
        src = pl.multiple_of(t * tile_rows, tile_rows)
        return pltpu.make_async_copy(h2_hbm.at[pl.ds(src, tile_rows)], hbuf.at[t % DISPATCH_RING],
                                     in_sem.at[t % DISPATCH_RING])

    def drain_scatters(t):
        for k in range(2):
            pltpu.make_async_copy(hbuf.at[t % DISPATCH_RING], xd_hbm.at[pl.ds(0, tile_rows)],
                                  out_sem.at[t % DISPATCH_RING]).wait()

    for t in range(DISPATCH_RING - 1):
        @pl.when((s == 0) & (t < n_steps))
        def _prime():
            tile_copy(t).start()

    tile_copy(s).wait()
    slot = s % DISPATCH_RING

    def body(r, carry):
        for k in range(2):
            dst = pl.multiple_of(dest_ref[k * ntok + s * tm + r], ROW_SUB)
            src = pl.multiple_of(r * ROW_SUB, ROW_SUB)
            pltpu.make_async_copy(hbuf.at[slot, pl.ds(src, ROW_SUB)], xd_hbm.at[pl.ds(dst, ROW_SUB)],
                                  out_sem.at[slot]).start(priority=k)
        return carry
    lax.fori_loop(0, tm, body, 0, unroll=DMA_UNROLL)

    @pl.when(s >= 1)
    def _drain_previous():
        drain_scatters(s - 1)

    @pl.when(s + DISPATCH_RING - 1 < n_steps)
    def _prefetch():
        tile_copy(s + DISPATCH_RING - 1).start()

    def zero_copy(row):
        dst = pl.multiple_of(row * ROW_SUB, ROW_SUB)
        return pltpu.make_async_copy(zero_buf, xd_hbm.at[pl.ds(dst, ROW_SUB)], zsem)

    n_ranges = pad_lo_ref.shape[0]
    for j in range(pads_per_step):
        idx = s * pads_per_step + j

        @pl.when(idx < n_ranges)
        def _fill():
            zero_buf[...] = jnp.zeros_like(zero_buf)
            lo = pad_lo_ref[idx]
            hi = pad_hi_ref[idx]
            lax.fori_loop(lo, hi, lambda row, c: (zero_copy(row).start(), c)[1], 0)
            lax.fori_loop(lo, hi, lambda row, c: (zero_copy(row).wait(), c)[1], 0)

    @pl.when(s == n_steps - 1)
    def _drain_last():
        drain_scatters(s)


def _dispatch(dest8, pad_lo, pad_hi, h2t, n_blocks):
    tm = TOK_TILE
    n_steps = h2t.shape[0] // (tm * ROW_SUB)
    xd_rows = n_blocks * MOE_BLK * ROW_SUB
    pads_per_step = -(-pad_lo.shape[0] // n_steps)
    return pl.pallas_call(
        functools.partial(_dispatch_kernel, pads_per_step=pads_per_step),
        grid_spec=pltpu.PrefetchScalarGridSpec(
            num_scalar_prefetch=3,
            grid=(n_steps,),
            in_specs=[pl.BlockSpec(memory_space=pl.ANY)],
            out_specs=pl.BlockSpec(memory_space=pl.ANY),
            scratch_shapes=[pltpu.VMEM((DISPATCH_RING, tm * ROW_SUB, LANES), F32),
                            pltpu.VMEM((ROW_SUB, LANES), F32),
                            pltpu.SemaphoreType.DMA((DISPATCH_RING,)),
                            pltpu.SemaphoreType.DMA((DISPATCH_RING,)),
                            pltpu.SemaphoreType.DMA(())],
        ),
        out_shape=jax.ShapeDtypeStruct((xd_rows, LANES), F32),
        compiler_params=pltpu.CompilerParams(
            dimension_semantics=("arbitrary",), vmem_limit_bytes=VMEM_LIMIT),
        name="moe_dispatch",
    )(dest8, pad_lo, pad_hi, h2t)


def _moe_kernel(blk_e_ref, nused_ref, xd_hbm, *refs):
    n = MOE_PER_STEP
    w_refs = [refs[3 * u:3 * u + 3] for u in range(n)]
    yd_ref = refs[3 * n]
    w_bf = [refs[3 * n + 1 + 3 * u:3 * n + 4 + 3 * u] for u in range(n)]
    xbuf, xsem = refs[6 * n + 1], refs[6 * n + 2]
    i = pl.program_id(0)
    nused = nused_ref[0]
    blk_rows = MOE_BLK * ROW_SUB
    ahead = MOE_RING - n

    def block_copy(j):
        src = pl.multiple_of(j * blk_rows, blk_rows)
        return pltpu.make_async_copy(xd_hbm.at[pl.ds(src, blk_rows)], xbuf.at[j % MOE_RING],
                                     xsem.at[j % MOE_RING])

    for j in range(ahead):
        @pl.when((i == 0) & (j < nused))
        def _prime():
            block_copy(j).start()

    for u in range(n):
        @pl.when(i * n + u + ahead < nused)
        def _prefetch():
            block_copy(i * n + u + ahead).start()

    for u in range(n):
        j = i * n + u
        wg_ref, wu_ref, wd_ref = w_refs[u]
        wg_s, wu_s, wd_s = w_bf[u]

        @pl.when((i == 0) | (blk_e_ref[j] != blk_e_ref[jnp.maximum(j - n, 0)]))
        def _cast_weights():
            wg_s[...] = wg_ref[0, 0].astype(BF16)
            wu_s[...] = wu_ref[0, 0].astype(BF16)
            wd_s[...] = wd_ref[0, 0].astype(BF16)

        out = yd_ref.at[pl.ds(u * blk_rows, blk_rows)]

        @pl.when(j < nused)
        def _compute():
            block_copy(j).wait()
            xb = _load_row_tiles(xbuf.at[j % MOE_RING], MOE_BLK).astype(BF16)
            gate = _dot(xb, wg_s[...])
            up = _dot(xb, wu_s[...])
            hid = (gate * _sigmoid(gate)) * up
            _store_row_tiles(out, _dot(hid.astype(BF16), wd_s[...]))

        @pl.when(j >= nused)
        def _unused():
            out[...] = jnp.zeros((blk_rows, LANES), F32)


def _moe(blk_e, nused, xd, wg, wu, wd, layer):
    nb = blk_e.shape[0]
    d = wg.shape[2]
    blk_rows = MOE_BLK * ROW_SUB
    n = MOE_PER_STEP
    assert nb % n == 0

    def w_spec(shape, u):
        return pl.BlockSpec(shape, lambda i, be, nu: (layer, be[i * n + u], 0, 0))

    w_specs = []
    w_scratch = []
    for u in range(n):
        w_specs += [w_spec((1, 1, d, D_EXPERT), u), w_spec((1, 1, d, D_EXPERT), u), w_spec((1, 1, D_EXPERT, d), u)]
        w_scratch += [pltpu.VMEM((d, D_EXPERT), BF16), pltpu.VMEM((d, D_EXPERT), BF16),
                      pltpu.VMEM((D_EXPERT, d), BF16)]
    return pl.pallas_call(
        _moe_kernel,
        grid_spec=pltpu.PrefetchScalarGridSpec(
            num_scalar_prefetch=2,
            grid=(nb // n,),
            in_specs=[pl.BlockSpec(memory_space=pl.ANY)] + w_specs,
            out_specs=pl.BlockSpec((n * blk_rows, LANES), lambda i, be, nu: (i, 0)),
            scratch_shapes=w_scratch + [pltpu.VMEM((MOE_RING, blk_rows, LANES), F32),
                                        pltpu.SemaphoreType.DMA((MOE_RING,))],
        ),
        out_shape=jax.ShapeDtypeStruct((nb * blk_rows, LANES), F32),
        compiler_params=pltpu.CompilerParams(
            dimension_semantics=("arbitrary",), vmem_limit_bytes=VMEM_LIMIT),
        name="moe_experts",
    )(blk_e, nused, xd, *([wg, wu, wd] * n))


def _combine_kernel(dest_ref, x_ref, mod_ref, g_ref, rt_ref, yd_hbm, o_ref, buf, sem, *, final, tile_of):
    s = pl.program_id(0)
    n_steps = pl.num_programs(0)
    tm = x_ref.shape[0]
    slot = s % 2
    ntok = dest_ref.shape[0] // 2

    def start_gather(step, dst_slot):
        base = tile_of(step) * tm

        def body(r, carry):
            for k in range(2):
                src = pl.multiple_of(dest_ref[k * ntok + base + r], ROW_SUB)
                dst = pl.multiple_of(r * ROW_SUB, ROW_SUB)
                pltpu.make_async_copy(yd_hbm.at[pl.ds(src, ROW_SUB)], buf.at[dst_slot, k, pl.ds(dst, ROW_SUB)],
                                      sem.at[dst_slot]).start(priority=k)
            return carry
        lax.fori_loop(0, tm, body, 0, unroll=DMA_UNROLL)

    @pl.when(s == 0)
    def _prime():
        start_gather(0, 0)

    @pl.when(s + 1 < n_steps)
    def _prefetch():
        start_gather(s + 1, 1 - slot)

    for k in range(2):
        pltpu.make_async_copy(yd_hbm.at[pl.ds(0, tm * ROW_SUB)], buf.at[slot, k], sem.at[slot]).wait()
    w0 = rt_ref[:, 2:3]
    w1 = rt_ref[:, 3:4]
    y = w0 * _load_row_tiles(buf.at[slot, 0], tm) + w1 * _load_row_tiles(buf.at[slot, 1], tm)
    x_new = x_ref[...] + mod_ref[0, 5:6, :] * y
    if final:
        ms = jnp.mean(x_new * x_new, axis=-1, keepdims=True)
        x_new = x_new * lax.rsqrt(ms + EPS) * g_ref[...]
    o_ref[...] = x_new


def _combine(dest, x_flat, mod8, g_final, rt, yd, n_lat, n_tiles, final):
    ntok, d = x_flat.shape
    tm = TOK_TILE
    bsz = ntok // (tm * n_tiles)
    n_per = n_lat if final else n_tiles
    n_steps = bsz * n_per

    def tile_of(step):
        return (step // n_per) * n_tiles + step % n_per

    def mod_map(s, dref):
        return (jnp.where(s % n_per < n_lat, s // n_per, bsz), 0, 0)

    return pl.pallas_call(
        functools.partial(_combine_kernel, final=final, tile_of=tile_of),
        grid_spec=pltpu.PrefetchScalarGridSpec(
            num_scalar_prefetch=1,
            grid=(n_steps,),
            in_specs=[
                pl.BlockSpec((tm, d), lambda s, dref: (tile_of(s), 0)),
                pl.BlockSpec((1, 8, d), mod_map),
                pl.BlockSpec((1, d), lambda s, dref: (0, 0)),
                pl.BlockSpec((tm, LANES), lambda s, dref: (tile_of(s), 0)),
                pl.BlockSpec(memory_space=pl.ANY),
            ],
            out_specs=pl.BlockSpec((tm, d), lambda s, dref: (s, 0)),
            scratch_shapes=[pltpu.VMEM((2, 2, tm * ROW_SUB, LANES), F32), pltpu.SemaphoreType.DMA((2,))],
        ),
        out_shape=jax.ShapeDtypeStruct((n_steps * tm, d), F32),
        compiler_params=pltpu.CompilerParams(
            dimension_semantics=("arbitrary",), vmem_limit_bytes=VMEM_LIMIT),
        name="combine",
    )(dest, x_flat, mod8, g_final, rt, yd)


def _rope_tables(seq, ctx_len):
    quarter = HEAD_DIM // 4
    rows = seq // GRID_W
    inv = ROPE_BASE ** (-jnp.arange(quarter, dtype=F32) / quarter)
    ang_r = jnp.arange(rows, dtype=F32)[:, None] * inv[None, :]
    ang_c = jnp.arange(GRID_W, dtype=F32)[:, None] * inv[None, :]

    def by_row(a):
        return jnp.broadcast_to(a[:, None, :], (rows, GRID_W, quarter)).reshape(seq, quarter)

    def by_col(a):
        return jnp.broadcast_to(a[None, :, :], (rows, GRID_W, quarter)).reshape(seq, quarter)

    cos_r, sin_r = by_row(jnp.cos(ang_r)), by_row(jnp.sin(ang_r))
    cos_c, sin_c = by_col(jnp.cos(ang_c)), by_col(jnp.sin(ang_c))
    cos = jnp.concatenate([cos_r, cos_r, cos_c, cos_c], axis=-1)
    sin = jnp.concatenate([-sin_r, sin_r, -sin_c, sin_c], axis=-1)
    cos = jnp.concatenate([cos, jnp.ones((ctx_len, HEAD_DIM), F32)], axis=0)
    sin = jnp.concatenate([sin, jnp.zeros((ctx_len, HEAD_DIM), F32)], axis=0)
    return jnp.concatenate([cos, cos], axis=-1), jnp.concatenate([sin, sin], axis=-1)


def _na_bias_tables(rpb, rows):
    qc = np.arange(GRID_W)
    kc = np.arange(GRID_W)
    cstart = np.clip(qc - WIN_C // 2, 0, GRID_W - WIN_C)
    col_ok = (kc[None, :] >= cstart[:, None]) & (kc[None, :] < cstart[:, None] + WIN_C)
    padw = GRID_W - WIN_C
    rpb_pad = jnp.pad(rpb, ((0, 0), (0, 0), (padw, padw)))
    toep = jnp.stack([rpb_pad[:, :, GRID_W - 1 - q:2 * GRID_W - 1 - q] for q in range(GRID_W)], axis=2)
    t = jnp.where(col_ok[None, None], toep, NEG_INF)
    masked = jnp.full((rpb.shape[0], GRID_W, GRID_W), NEG_INF, F32)
    n_blk = rows // NA_ROWS
    cases = []
    for blk in (0, min(1, n_blk - 1), n_blk - 1):
        r0 = blk * NA_ROWS
        ks = int(np.clip(r0 - WIN_R // 2, 0, rows - NA_KROWS))
        q_rows = []
        for qr in range(NA_ROWS):
            r = r0 + qr
            rs = int(np.clip(r - WIN_R // 2, 0, rows - WIN_R))
            tiles = []
            for a in range(NA_KROWS):
                kr = ks + a
                tiles.append(t[:, kr - r + WIN_R - 1] if rs <= kr < rs + WIN_R else masked)
            q_rows.append(jnp.concatenate(tiles, axis=-1))
        cases.append(jnp.concatenate(q_rows, axis=-2))
    return jnp.stack(cases, axis=0).astype(F32)


def _routing_plan(rtt, cnt, n_blocks):
    counts = cnt[0, 8:8 + N_EXPERTS].astype(jnp.int32)
    padded = ((counts + MOE_BLK - 1) // MOE_BLK) * MOE_BLK
    pad_end = jnp.cumsum(padded)
    pad_start = pad_end - padded
    eid = rtt[0:2].astype(jnp.int32)
    pos = rtt[4:6].astype(jnp.int32)
    start_of = jnp.zeros_like(eid)
    for e in range(N_EXPERTS):
        start_of = jnp.where(eid == e, pad_start[e], start_of)
    dest8 = ((start_of + pos) * ROW_SUB).reshape(-1)
    blk_start = jnp.arange(n_blocks, dtype=jnp.int32) * MOE_BLK
    blk_e = jnp.clip(jnp.sum(pad_end[None, :] <= blk_start[:, None], axis=1), 0, N_EXPERTS - 1).astype(jnp.int32)
    nused = (pad_end[-1] // MOE_BLK).astype(jnp.int32).reshape(1)
    total = jnp.full((1,), n_blocks * MOE_BLK, jnp.int32)
    pad_lo = jnp.concatenate([pad_start + counts, pad_end[-1:]]).astype(jnp.int32)
    pad_hi = jnp.concatenate([pad_end, total]).astype(jnp.int32)
    return blk_e, nused, dest8, pad_lo, pad_hi


def kernel(x, c, ctx, c_ctx, w_mod, b_mod, g_attn, g_ffn, w_in, conv_w, rpb, sinks, w_branch, w_o,
           w_rg, b_rg, w_re, b_re, w_e_gate, w_e_up, w_e_down, g_final):
    bsz, seq, d = x.shape
    ctx_len = ctx.shape[1]
    depth = w_in.shape[0]
    ltot = seq + ctx_len
    n_lat = seq // TOK_TILE
    n_tiles = ltot // TOK_TILE
    rows = seq // GRID_W
    ntok = bsz * ltot
    n_assign = ntok * 2
    n_blocks = (n_assign + N_EXPERTS * (MOE_BLK - 1) + MOE_BLK - 1) // MOE_BLK

    x_all = jnp.concatenate([x, ctx], axis=1)
    cos_t, sin_t = _rope_tables(seq, ctx_len)
    assert bsz + 1 <= 8
    c8 = jnp.pad(jnp.concatenate([c, c_ctx[None, :]], axis=0), ((0, 7 - bsz), (0, 0)))
    mod_all = _mod_vectors(c8, w_mod, b_mod)
    w_in_bf = w_in.astype(BF16)
    w_branch_bf = w_branch.astype(BF16)
    w_o_bf = w_o.astype(BF16)

    for l in range(depth):
        mod8 = jnp.pad(mod_all[l, :bsz + 1].reshape(bsz + 1, 6, d), ((0, 0), (0, 2), (0, 0)))
        bias = _na_bias_tables(rpb[l], rows)
        o2 = D_CONV + D_NA
        wb = jnp.concatenate([w_branch_bf[l, :o2]] + [w_branch_bf[l, o2 + h * HEAD_DIM:o2 + (h + 1) * HEAD_DIM]
                                                      for h in _SW_HEAD_ORDER], axis=0)
        pad_g = jnp.zeros((d, 8 - N_GROUPS), F32)
        pad_e = jnp.zeros((d, LANES - 8 - N_EXPERTS), F32)
        wr = jnp.concatenate([w_rg[l], pad_g, w_re[l], pad_e], axis=1)
        wr_hi = wr.astype(BF16)
        wr_lo = (wr - wr_hi.astype(F32)).astype(BF16)
        wr_cat = jnp.concatenate([wr_hi, wr_lo], axis=1)
        br =jnp.concatenate([b_rg[l], pad_g[0], b_re[l], pad_e[0]])[None, :]
        convw8 = jnp.pad(conv_w[l], ((0, 8 - CONV_W), (0, 0)))

        z, ab, nq, nk, nv, wq, wk, wv, gates = _inproj(
            x_all, mod8, g_attn[l][None, :], w_in_bf, l, cos_t, sin_t, n_lat)
        y_b = _na_attn(nq, nk, nv, bias, seq)
        y_c = _sw_attn(sinks[l], wq, wk, wv, seq)
        x_mid, h2t, rt, rtt, cnt = _merge(x_all, mod8, g_ffn[l][None, :], z, ab, y_b, y_c, gates, convw8, wb,
                                          w_o_bf[l], wr_cat, br, n_lat)
        rt = rt.reshape(ntok, LANES)
        blk_e, nused, dest8, pad_lo, pad_hi = _routing_plan(rtt, cnt, n_blocks)
        xd = _dispatch(dest8, pad_lo, pad_hi, h2t, n_blocks)
        yd = _moe(blk_e, nused, xd, w_e_gate, w_e_up, w_e_down, l)
        final = l == depth - 1
        x_flat = _combine(dest8, x_mid.reshape(ntok, d), mod8, g_final[None, :], rt, yd, n_lat, n_tiles,
                          final)
        if not final:
            x_all = x_flat.reshape(bsz, ltot, d)
    return x_flat.reshape(bsz, seq, d)
```
